```python
import math
import jax, jax.numpy as jnp
from jax import lax
import numpy as np

D_MODEL = 2048
BATCH = 4
SEQ = 4096
DEPTH = 4

N_HEADS = 16
HEAD_DIM = 128
N_KV_HEADS = 4
Q_GROUP = N_HEADS // N_KV_HEADS
ATT_WIDTH = N_HEADS * HEAD_DIM
KV_WIDTH = N_KV_HEADS * HEAD_DIM
MOBA_BLOCK = 256
MOBA_TOPK = 3
Q_CHUNK = 128
SWA_WINDOW = 128
SWA_BLOCK = 128
NUM_BUCKETS = 32
MAX_DISTANCE = 128
D_FF = 5504
CONV_WIDTH = 3
N_A_LAYERS = DEPTH // 2
N_B_LAYERS = DEPTH - N_A_LAYERS
ALPHA = (2.0 * DEPTH) ** 0.25
BETA = (8.0 * DEPTH) ** -0.25
LN_EPS = 1e-5
NEG = -1e30

kernel_name = "yoco_moba_swa_convffn_deepnorm"


def t5_bucket(dist):
    n = jnp.maximum(dist, 0)
    max_exact = NUM_BUCKETS // 2
    nf = jnp.maximum(n, 1).astype(jnp.float32)
    large = max_exact + (jnp.log(nf / max_exact) / math.log(MAX_DISTANCE / max_exact)
                         * (NUM_BUCKETS - max_exact)).astype(jnp.int32)
    large = jnp.minimum(large, NUM_BUCKETS - 1)
    return jnp.where(n < max_exact, n, large)


def layer_norm(x, g, b):
    xf = x.astype(jnp.float32)
    mu = jnp.mean(xf, -1, keepdims=True)
    var = jnp.mean(jnp.square(xf - mu), -1, keepdims=True)
    return ((xf - mu) * lax.rsqrt(var + LN_EPS) * g.astype(jnp.float32)
            + b.astype(jnp.float32)).astype(x.dtype)


def conv_ffn(x, w_up, conv_w, conv_b, w_down):
    h = x @ w_up
    c = h.shape[-1]
    hp = jnp.pad(h, ((0, 0), (CONV_WIDTH - 1, 0), (0, 0)))
    hc = lax.conv_general_dilated(hp, conv_w.reshape(CONV_WIDTH, 1, c).astype(hp.dtype),
                                  window_strides=(1,), padding='VALID',
                                  dimension_numbers=('NWC', 'WIO', 'NWC'),
                                  feature_group_count=c) + conv_b
    g, v = jnp.split(hc, 2, axis=-1)
    return (jax.nn.gelu(g) * v) @ w_down


def moba_attention(x, w_qkv, w_o, rel_bias):
    B, S, _ = x.shape
    q, k, v = jnp.split(x @ w_qkv, 3, axis=-1)
    to_heads = lambda t: t.reshape(B, S, N_HEADS, HEAD_DIM).transpose(0, 2, 1, 3)
    q, k, v = to_heads(q), to_heads(k), to_heads(v)
    n_blocks = -(-S // MOBA_BLOCK)
    pad = n_blocks * MOBA_BLOCK - S
    kb = jnp.pad(k, ((0, 0), (0, 0), (0, pad), (0, 0))).reshape(B, N_HEADS, n_blocks, MOBA_BLOCK, HEAD_DIM)
    vb = jnp.pad(v, ((0, 0), (0, 0), (0, pad), (0, 0))).reshape(B, N_HEADS, n_blocks, MOBA_BLOCK, HEAD_DIM)
    k_mean = jnp.mean(kb.astype(jnp.float32), axis=3)
    n_slots = min(MOBA_TOPK, n_blocks - 1)
    scale = HEAD_DIM ** -0.5
    bias_table = rel_bias.T
    b_ix = jnp.arange(B)[:, None, None]
    h_ix = jnp.arange(N_HEADS)[None, :, None]
    h_ix4 = h_ix[..., None]
    blk_ar = jnp.arange(MOBA_BLOCK)

    def chunk(c):
        q0 = c * Q_CHUNK
        qc = lax.dynamic_slice_in_dim(q, q0, Q_CHUNK, axis=2)
        q_pos = q0 + jnp.arange(Q_CHUNK)
        own = q0 // MOBA_BLOCK
        k_own = lax.dynamic_index_in_dim(kb, own, axis=2, keepdims=False)
        v_own = lax.dynamic_index_in_dim(vb, own, axis=2, keepdims=False)
        dist_own = q_pos[:, None] - (own * MOBA_BLOCK + blk_ar)[None, :]
        l_own = (jnp.einsum('bhqd,bhkd->bhqk', qc, k_own).astype(jnp.float32) * scale
                 + bias_table[:, t5_bucket(dist_own)].astype(jnp.float32))
        logits = [jnp.where(dist_own >= 0, l_own, NEG)]
        ids_list = []
        if n_slots > 0:
            gate = jnp.einsum('bhqd,bhnd->bhqn', qc.astype(jnp.float32), k_mean)
            gate = jnp.where(jnp.arange(n_blocks) < own, gate, NEG)
            _, idx = lax.top_k(gate, n_slots)
            valid = idx < own
            for s in range(n_slots):
                ids = idx[..., s]
                kg = kb[b_ix, h_ix, ids]
                l_s = jnp.einsum('bhqd,bhqkd->bhqk', qc, kg).astype(jnp.float32) * scale
                dist = q_pos[None, None, :, None] - (ids[..., None] * MOBA_BLOCK + blk_ar)
                l_s = l_s + bias_table[h_ix4, t5_bucket(dist)].astype(jnp.float32)
                logits.append(jnp.where(valid[..., s, None], l_s, NEG))
                ids_list.append(ids)
        p = jax.nn.softmax(jnp.concatenate(logits, axis=-1), axis=-1).astype(v.dtype)
        out = jnp.einsum('bhqk,bhkd->bhqd', p[..., :MOBA_BLOCK], v_own)
        for s, ids in enumerate(ids_list):
            vg = vb[b_ix, h_ix, ids]
            p_s = p[..., (s + 1) * MOBA_BLOCK:(s + 2) * MOBA_BLOCK]
            out = out + jnp.einsum('bhqk,bhqkd->bhqd', p_s, vg)
        return out

    outs = lax.map(chunk, jnp.arange(S // Q_CHUNK))
    o = outs.transpose(1, 0, 3, 2, 4).reshape(B, S, ATT_WIDTH)
    return o @ w_o


def shared_kv_bands(h, w_kv):
    B, S, _ = h.shape
    nq = S // SWA_BLOCK
    k, v = jnp.split(h @ w_kv, 2, axis=-1)
    def band(t):
        tb = t.reshape(B, nq, SWA_BLOCK, N_KV_HEADS, HEAD_DIM)
        prev = jnp.concatenate([jnp.zeros_like(tb[:, :1]), tb[:, :-1]], axis=1)
        return jnp.concatenate([prev, tb], axis=2)
    return band(k), band(v)


def swa_attention(x, w_q, sinks, w_o, k_band, v_band, band_bias, band_mask):
    B, S, _ = x.shape
    nq = S // SWA_BLOCK
    q = (x @ w_q).reshape(B, nq, SWA_BLOCK, N_KV_HEADS, Q_GROUP, HEAD_DIM)
    logits = jnp.einsum('bnqhgd,bnjhd->bhgnqj', q, k_band).astype(jnp.float32) * (HEAD_DIM ** -0.5)
    logits = logits + band_bias
    logits = jnp.where(band_mask, logits, NEG)
    sink = jnp.broadcast_to(sinks.astype(jnp.float32).reshape(N_KV_HEADS, Q_GROUP, 1, 1, 1),
                            logits.shape[:-1] + (1,))
    p = jax.nn.softmax(jnp.concatenate([logits, sink], axis=-1), axis=-1)[..., :-1]
    o = jnp.einsum('bhgnqj,bnjhd->bnqhgd', p.astype(v_band.dtype), v_band)
    return o.reshape(B, S, ATT_WIDTH) @ w_o


def setup_inputs(seed: int = 0) -> dict:
    key = jax.random.key(seed)
    ks = jax.random.split(key, 16)
    f32 = jnp.float32
    nrm = lambda k, shape, s: jax.random.normal(k, shape, f32) * s
    x = nrm(ks[0], (BATCH, SEQ, D_MODEL), 1.0)
    rel_bias = nrm(ks[1], (NUM_BUCKETS, N_HEADS), 0.5)
    qkv_scale = jnp.concatenate([jnp.ones((2 * ATT_WIDTH,), f32), jnp.full((ATT_WIDTH,), BETA, f32)])
    moba_w_qkv = nrm(ks[2], (N_A_LAYERS, D_MODEL, 3 * ATT_WIDTH), D_MODEL ** -0.5) * qkv_scale
    moba_w_o = nrm(ks[3], (N_A_LAYERS, ATT_WIDTH, D_MODEL), BETA * ATT_WIDTH ** -0.5)
    kv_scale = jnp.concatenate([jnp.ones((KV_WIDTH,), f32), jnp.full((KV_WIDTH,), BETA, f32)])
    swa_w_kv = nrm(ks[4], (D_MODEL, 2 * KV_WIDTH), D_MODEL ** -0.5) * kv_scale
    swa_w_q = nrm(ks[5], (N_B_LAYERS, D_MODEL, ATT_WIDTH), D_MODEL ** -0.5)
    swa_sinks = nrm(ks[6], (N_B_LAYERS, N_HEADS), 1.0)
    swa_w_o = nrm(ks[7], (N_B_LAYERS, ATT_WIDTH, D_MODEL), BETA * ATT_WIDTH ** -0.5)
    ffn_w_up = nrm(ks[8], (DEPTH, D_MODEL, 2 * D_FF), D_MODEL ** -0.5)
    ffn_conv_w = nrm(ks[9], (DEPTH, CONV_WIDTH, 2 * D_FF), CONV_WIDTH ** -0.5)
    ffn_conv_b = nrm(ks[10], (DEPTH, 2 * D_FF), 0.01)
    ffn_w_down = nrm(ks[11], (DEPTH, D_FF, D_MODEL), BETA * D_FF ** -0.5)
    ln_g = 1.0 + nrm(ks[12], (DEPTH, 2, D_MODEL), 0.02)
    ln_b = nrm(ks[13], (DEPTH, 2, D_MODEL), 0.02)
    return {"x": x, "rel_bias": rel_bias, "moba_w_qkv": moba_w_qkv, "moba_w_o": moba_w_o,
            "swa_w_kv": swa_w_kv, "swa_w_q": swa_w_q, "swa_sinks": swa_sinks, "swa_w_o": swa_w_o,
            "ffn_w_up": ffn_w_up, "ffn_conv_w": ffn_conv_w, "ffn_conv_b": ffn_conv_b,
            "ffn_w_down": ffn_w_down, "ln_g": ln_g, "ln_b": ln_b}


def reference(x, rel_bias, moba_w_qkv, moba_w_o, swa_w_kv, swa_w_q, swa_sinks, swa_w_o,
              ffn_w_up, ffn_conv_w, ffn_conv_b, ffn_w_down, ln_g, ln_b):
    S = x.shape[1]
    nq = S // SWA_BLOCK
    qi = jnp.arange(SWA_BLOCK)[:, None]
    kj = jnp.arange(2 * SWA_BLOCK)[None, :]
    dist = qi + SWA_BLOCK - kj
    in_band = (dist >= 0) & (dist < SWA_WINDOW)
    key_exists = (jnp.arange(nq)[:, None, None] * SWA_BLOCK + kj[None] - SWA_BLOCK) >= 0
    band_mask = in_band[None] & key_exists
    band_bias = rel_bias[t5_bucket(dist)].astype(jnp.float32).transpose(2, 0, 1)
    band_bias = band_bias.reshape(N_KV_HEADS, Q_GROUP, 1, SWA_BLOCK, 2 * SWA_BLOCK)

    h = x
    k_band = v_band = None
    for layer in range(DEPTH):
        if layer < N_A_LAYERS:
            attn = moba_attention(h, moba_w_qkv[layer], moba_w_o[layer], rel_bias)
        else:
            if layer == N_A_LAYERS:
                k_band, v_band = shared_kv_bands(h, swa_w_kv)
            j = layer - N_A_LAYERS
            attn = swa_attention(h, swa_w_q[j], swa_sinks[j], swa_w_o[j],
                                 k_band, v_band, band_bias, band_mask)
        h = layer_norm(ALPHA * h + attn, ln_g[layer, 0], ln_b[layer, 0])
        ffn = conv_ffn(h, ffn_w_up[layer], ffn_conv_w[layer], ffn_conv_b[layer], ffn_w_down[layer])
        h = layer_norm(ALPHA * h + ffn, ln_g[layer, 1], ln_b[layer, 1])
    return h
```

```python
import functools
import math

import jax
import jax.numpy as jnp
from jax import lax
from jax.experimental import pallas as pl
from jax.experimental.pallas import tpu as pltpu

D_MODEL = 2048
N_HEADS = 16
HEAD_DIM = 128
N_KV_HEADS = 4
Q_GROUP = N_HEADS // N_KV_HEADS
ATT_WIDTH = N_HEADS * HEAD_DIM
KV_WIDTH = N_KV_HEADS * HEAD_DIM
MOBA_BLOCK = 256
MOBA_TOPK = 3
SWA_WINDOW = 128
SWA_BLOCK = 128
NUM_BUCKETS = 32
MAX_DISTANCE = 128
D_FF = 5504
CONV_WIDTH = 3
DEPTH = 4
N_A_LAYERS = DEPTH // 2
ALPHA = (2.0 * DEPTH) ** 0.25
LN_EPS = 1e-5
NEG = -1e30
BIG = 1e30
SCALE = HEAD_DIM ** -0.5

V7X_VMEM_BYTES = 64 * 1024 * 1024
VMEM_LIMIT = 56 * 1024 * 1024
SUBLANES = 8
FF_TILE = 512
D_FF_PAD = -(-D_FF // FF_TILE) * FF_TILE
FFN_ROWS = 512
PROJ_ROWS = 1024
PROJ_COLS = 1024
LN_ROWS = 512

_NT = (((1,), (1,)), ((), ()))


def _t5_bucket(dist):
    n = jnp.maximum(dist, 0)
    max_exact = NUM_BUCKETS // 2
    nf = jnp.maximum(n, 1).astype(jnp.float32)
    large = max_exact + (jnp.log(nf / max_exact) / math.log(MAX_DISTANCE / max_exact)
                         * (NUM_BUCKETS - max_exact)).astype(jnp.int32)
    large = jnp.minimum(large, NUM_BUCKETS - 1)
    return jnp.where(n < max_exact, n, large)


def _layer_norm(y, g, b):
    mu = jnp.mean(y, axis=-1, keepdims=True)
    yc = y - mu
    var = jnp.mean(yc * yc, axis=-1, keepdims=True)
    return yc * lax.rsqrt(var + LN_EPS) * g + b


def _proj_kernel(h_ref, w_ref, o_ref, xb_ref):
    @pl.when(pl.program_id(1) == 0)
    def _():
        xb_ref[...] = h_ref[...].astype(jnp.bfloat16)

    o_ref[...] = jnp.dot(xb_ref[...], w_ref[...],
                         preferred_element_type=jnp.float32).astype(o_ref.dtype)


def _proj(h, w):
    t, d = h.shape
    n = w.shape[1]
    tm, tn = PROJ_ROWS, min(PROJ_COLS, n)
    return pl.pallas_call(
        _proj_kernel,
        grid=(t // tm, n // tn),
        in_specs=[pl.BlockSpec((tm, d), lambda i, j: (i, 0)),
                  pl.BlockSpec((d, tn), lambda i, j: (0, j))],
        out_specs=pl.BlockSpec((tm, tn), lambda i, j: (i, j)),
        out_shape=jax.ShapeDtypeStruct((t, n), jnp.bfloat16),
        scratch_shapes=[pltpu.VMEM((tm, d), jnp.bfloat16)],
        compiler_params=pltpu.CompilerParams(
            dimension_semantics=("parallel", "arbitrary"), vmem_limit_bytes=VMEM_LIMIT),
    )(h, w)


def _proj_ln_kernel(a_ref, w_ref, h_ref, g_ref, b_ref, o_ref):
    y = ALPHA * h_ref[...] + jnp.dot(a_ref[...], w_ref[...], preferred_element_type=jnp.float32)
    o_ref[...] = _layer_norm(y, g_ref[...], b_ref[...])


def _proj_ln(a, w, h, g, b):
    t, k = a.shape
    d = w.shape[1]
    tm = LN_ROWS
    return pl.pallas_call(
        _proj_ln_kernel,
        grid=(t // tm,),
        in_specs=[pl.BlockSpec((tm, k), lambda i: (i, 0)),
                  pl.BlockSpec((k, d), lambda i: (0, 0)),
                  pl.BlockSpec((tm, d), lambda i: (i, 0)),
                  pl.BlockSpec((1, d), lambda i: (0, 0)),
                  pl.BlockSpec((1, d), lambda i: (0, 0))],
        out_specs=pl.BlockSpec((tm, d), lambda i: (i, 0)),
        out_shape=jax.ShapeDtypeStruct((t, d), jnp.float32),
        compiler_params=pltpu.CompilerParams(
            dimension_semantics=("parallel",), vmem_limit_bytes=VMEM_LIMIT),
    )(a, w, h, g, b)


def _ffn_kernel(h_ref, halo_ref, wg_ref, wv_ref, cwg_ref, cwv_ref, cbg_ref, cbv_ref, wd_ref,
                g_ref, b_ref, o_ref, x_ref, acc_ref, *, seq_len):
    i = pl.program_id(0)
    f = pl.program_id(1)
    tm = h_ref.shape[0]

    @pl.when(f == 0)
    def _():
        at_start = (i * tm) % seq_len == 0
        halo = jnp.where(at_start, 0.0, halo_ref[...])
        x_ref[0:SUBLANES, :] = halo.astype(jnp.bfloat16)
        x_ref[SUBLANES:, :] = h_ref[...].astype(jnp.bfloat16)
        acc_ref[...] = jnp.zeros_like(acc_ref)

    x = x_ref[...]

    def up_conv(w_ref, cw_ref, cb_ref):
        u = jnp.dot(x, w_ref[...], preferred_element_type=jnp.float32)
        u1 = pltpu.roll(u, 1, 0)
        u2 = pltpu.roll(u, 2, 0)
        cw = cw_ref[...]
        c = (u2[SUBLANES:] * cw[0:1] + u1[SUBLANES:] * cw[1:2] + u[SUBLANES:] * cw[2:3]
             + cb_ref[...])
        return c

    gate = up_conv(wg_ref, cwg_ref, cbg_ref)
    val = up_conv(wv_ref, cwv_ref, cbv_ref)
    act = (jax.nn.gelu(gate) * val).astype(jnp.bfloat16)
    acc_ref[...] += jnp.dot(act, wd_ref[...], preferred_element_type=jnp.float32)

    @pl.when(f == pl.num_programs(1) - 1)
    def _():
        y = ALPHA * h_ref[...] + acc_ref[...]
        o_ref[...] = _layer_norm(y, g_ref[...], b_ref[...])


def _conv_ffn_ln(h, w_up, conv_w, conv_b, w_down, g, b, seq_len):
    t, d = h.shape
    tm, tf = FFN_ROWS, FF_TILE
    nf = D_FF_PAD // tf
    halo_blocks = tm // SUBLANES
    return pl.pallas_call(
        functools.partial(_ffn_kernel, seq_len=seq_len),
        grid=(t // tm, nf),
        in_specs=[pl.BlockSpec((tm, d), lambda i, f: (i, 0)),
                  pl.BlockSpec((SUBLANES, d), lambda i, f: (jnp.maximum(i * halo_blocks - 1, 0), 0)),
                  pl.BlockSpec((d, tf), lambda i, f: (0, f)),
                  pl.BlockSpec((d, tf), lambda i, f: (0, nf + f)),
                  pl.BlockSpec((CONV_WIDTH, tf), lambda i, f: (0, f)),
                  pl.BlockSpec((CONV_WIDTH, tf), lambda i, f: (0, nf + f)),
                  pl.BlockSpec((1, tf), lambda i, f: (0, f)),
                  pl.BlockSpec((1, tf), lambda i, f: (0, nf + f)),
                  pl.BlockSpec((tf, d), lambda i, f: (f, 0)),
                  pl.BlockSpec((1, d), lambda i, f: (0, 0)),
                  pl.BlockSpec((1, d), lambda i, f: (0, 0))],
        out_specs=pl.BlockSpec((tm, d), lambda i, f: (i, 0)),
        out_shape=jax.ShapeDtypeStruct((t, d), jnp.float32),
        scratch_shapes=[pltpu.VMEM((tm + SUBLANES, d), jnp.bfloat16),
                        pltpu.VMEM((tm, d), jnp.float32)],
        compiler_params=pltpu.CompilerParams(
            dimension_semantics=("parallel", "arbitrary"), vmem_limit_bytes=VMEM_LIMIT),
    )(h, h, w_up, w_up, conv_w, conv_w, conv_b, conv_b, w_down, g, b)


def _moba_kernel(far_ref, q_ref, k_ref, v_ref, bown_ref, bprev_ref, o_ref,
                 kmean_ref, vt_ref, sel_ref, m_ref, l_ref, acc_ref):
    hd = pl.program_id(1)
    c = pl.program_id(2)
    n_blocks = kmean_ref.shape[0]
    blk = MOBA_BLOCK

    @pl.when(c == 0)
    def _():
        rows = lax.broadcasted_iota(jnp.int32, (HEAD_DIM, HEAD_DIM), 0)
        cols = lax.broadcasted_iota(jnp.int32, (HEAD_DIM, HEAD_DIM), 1)
        eye = (rows == cols).astype(jnp.bfloat16)
        for n in range(n_blocks):
            kb = k_ref[0, n * blk:(n + 1) * blk, :]
            vb = v_ref[0, n * blk:(n + 1) * blk, :]
            kmean_ref[n:n + 1, :] = jnp.sum(kb.astype(jnp.float32), axis=0, keepdims=True) * (1.0 / blk)
            vt_ref[n] = lax.dot_general(eye, vb, _NT,
                                        preferred_element_type=jnp.float32).astype(jnp.bfloat16)

    q = q_ref[0]

    gate = lax.dot_general(kmean_ref[...], q.astype(jnp.float32), _NT,
                           precision=lax.Precision.HIGHEST, preferred_element_type=jnp.float32)
    row = lax.broadcasted_iota(jnp.int32, gate.shape, 0)
    past = row < c
    gate = jnp.where(past, gate, NEG)
    rank = jnp.zeros(gate.shape, jnp.int32)
    for m in range(n_blocks):
        gm = gate[m:m + 1, :]
        beats = (gm > gate) | ((gm == gate) & (row > m))
        rank = rank + beats.astype(jnp.int32)
    sel_ref[...] = ((rank < MOBA_TOPK) & past).astype(jnp.float32)

    def scores(n):
        kb = k_ref[0, pl.ds(pl.multiple_of(n * blk, blk), blk), :]
        return lax.dot_general(kb, q, _NT, preferred_element_type=jnp.float32) * SCALE

    s = scores(c) + bown_ref[0]
    m0 = jnp.max(s, axis=0, keepdims=True)
    p = jnp.exp(s - m0)
    m_ref[...] = m0
    l_ref[...] = jnp.sum(p, axis=0, keepdims=True)
    acc_ref[...] = jnp.dot(vt_ref[c], p.astype(jnp.bfloat16), preferred_element_type=jnp.float32)

    def update(n, s):
        chosen = sel_ref[pl.ds(n, 1), :] > 0.5
        m_old = m_ref[...]
        m_new = jnp.where(chosen, jnp.maximum(m_old, jnp.max(s, axis=0, keepdims=True)), m_old)
        p = jnp.exp(s - jnp.where(chosen, m_new, BIG))
        alpha = jnp.exp(m_old - m_new)
        m_ref[...] = m_new
        l_ref[...] = alpha * l_ref[...] + jnp.sum(p, axis=0, keepdims=True)
        acc_ref[...] = alpha * acc_ref[...] + jnp.dot(
            vt_ref[n], p.astype(jnp.bfloat16), preferred_element_type=jnp.float32)

    @pl.when(c >= 1)
    def _():
        update(c - 1, scores(c - 1) + bprev_ref[0])

    far = far_ref[hd]

    def body(n, carry):
        update(n, scores(n) + far)
        return carry

    lax.fori_loop(0, c - 1, body, 0)

    out_t = acc_ref[...] / l_ref[...]
    o_ref[0] = out_t.T.astype(o_ref.dtype)


def _moba_attention(qkv, bias_own, bias_prev, bias_far):
    bsz, s, _ = qkv.shape
    nb = s // MOBA_BLOCK
    return pl.pallas_call(
        _moba_kernel,
        grid=(bsz, N_HEADS, nb),
        in_specs=[pl.BlockSpec(memory_space=pltpu.SMEM),
                  pl.BlockSpec((1, MOBA_BLOCK, HEAD_DIM), lambda b, h, c: (b, c, h)),
                  pl.BlockSpec((1, s, HEAD_DIM), lambda b, h, c: (b, 0, N_HEADS + h)),
                  pl.BlockSpec((1, s, HEAD_DIM), lambda b, h, c: (b, 0, 2 * N_HEADS + h)),
                  pl.BlockSpec((1, MOBA_BLOCK, MOBA_BLOCK), lambda b, h, c: (h, 0, 0)),
                  pl.BlockSpec((1, MOBA_BLOCK, MOBA_BLOCK), lambda b, h, c: (h, 0, 0))],
        out_specs=pl.BlockSpec((1, MOBA_BLOCK, HEAD_DIM), lambda b, h, c: (b, c, h)),
        out_shape=jax.ShapeDtypeStruct((bsz, s, ATT_WIDTH), jnp.bfloat16),
        scratch_shapes=[pltpu.VMEM((nb, HEAD_DIM), jnp.float32),
                        pltpu.VMEM((nb, HEAD_DIM, MOBA_BLOCK), jnp.bfloat16),
                        pltpu.VMEM((nb, MOBA_BLOCK), jnp.float32),
                        pltpu.VMEM((1, MOBA_BLOCK), jnp.float32),
                        pltpu.VMEM((1, MOBA_BLOCK), jnp.float32),
                        pltpu.VMEM((HEAD_DIM, MOBA_BLOCK), jnp.float32)],
        compiler_params=pltpu.CompilerParams(
            dimension_semantics=("parallel", "parallel", "arbitrary"), vmem_limit_bytes=VMEM_LIMIT),
    )(bias_far, qkv, qkv, qkv, bias_own, bias_prev)


def _moba_bias_tables(rel_bias):
    key = jnp.arange(MOBA_BLOCK)[:, None]
    qry = jnp.arange(MOBA_BLOCK)[None, :]
    table = rel_bias.astype(jnp.float32).T
    d_own = qry - key
    own = jnp.where(d_own >= 0, table[:, _t5_bucket(d_own)], NEG)
    prev = table[:, _t5_bucket(d_own + MOBA_BLOCK)]
    far = table[:, NUM_BUCKETS - 1]
    return own, prev, far


def _swa_kernel(sink_ref, q_ref, kp_ref, kc_ref, vp_ref, vc_ref, bias_ref, o_ref):
    g = pl.program_id(1)
    c = pl.program_id(2)
    kp, kc, vp, vc = kp_ref[0], kc_ref[0], vp_ref[0], vc_ref[0]
    has_prev = c > 0
    for gi in range(Q_GROUP):
        q = q_ref[0, :, gi * HEAD_DIM:(gi + 1) * HEAD_DIM]
        bias = bias_ref[gi]
        s_prev = lax.dot_general(q, kp, _NT, preferred_element_type=jnp.float32) * SCALE \
            + bias[:, :SWA_BLOCK]
        s_prev = jnp.where(has_prev, s_prev, NEG)
        s_cur = lax.dot_general(q, kc, _NT, preferred_element_type=jnp.float32) * SCALE \
            + bias[:, SWA_BLOCK:]
        sink = sink_ref[g * Q_GROUP + gi]
        m = jnp.maximum(jnp.maximum(jnp.max(s_prev, axis=1, keepdims=True),
                                    jnp.max(s_cur, axis=1, keepdims=True)), sink)
        p_prev = jnp.exp(s_prev - m)
        p_cur = jnp.exp(s_cur - m)
        denom = (jnp.sum(p_prev, axis=1, keepdims=True) + jnp.sum(p_cur, axis=1, keepdims=True)
                 + jnp.exp(sink - m))
        o = (jnp.dot(p_prev.astype(jnp.bfloat16), vp, preferred_element_type=jnp.float32)
             + jnp.dot(p_cur.astype(jnp.bfloat16), vc, preferred_element_type=jnp.float32))
        o_ref[0, :, gi * HEAD_DIM:(gi + 1) * HEAD_DIM] = (o / denom).astype(o_ref.dtype)


def _swa_attention(q, kv, band_bias, sinks):
    bsz, s, _ = q.shape
    nq = s // SWA_BLOCK
    gw = Q_GROUP * HEAD_DIM
    prev = lambda c: jnp.maximum(c - 1, 0)
    return pl.pallas_call(
        _swa_kernel,
        grid=(bsz, N_KV_HEADS, nq),
        in_specs=[pl.BlockSpec(memory_space=pltpu.SMEM),
                  pl.BlockSpec((1, SWA_BLOCK, gw), lambda b, g, c: (b, c, g)),
                  pl.BlockSpec((1, SWA_BLOCK, HEAD_DIM), lambda b, g, c: (b, prev(c), g)),
                  pl.BlockSpec((1, SWA_BLOCK, HEAD_DIM), lambda b, g, c: (b, c, g)),
                  pl.BlockSpec((1, SWA_BLOCK, HEAD_DIM), lambda b, g, c: (b, prev(c), N_KV_HEADS + g)),
                  pl.BlockSpec((1, SWA_BLOCK, HEAD_DIM), lambda b, g, c: (b, c, N_KV_HEADS + g)),
                  pl.BlockSpec((Q_GROUP, SWA_BLOCK, 2 * SWA_BLOCK), lambda b, g, c: (g, 0, 0))],
        out_specs=pl.BlockSpec((1, SWA_BLOCK, gw), lambda b, g, c: (b, c, g)),
        out_shape=jax.ShapeDtypeStruct((bsz, s, ATT_WIDTH), jnp.bfloat16),
        compiler_params=pltpu.CompilerParams(
            dimension_semantics=("parallel", "parallel", "arbitrary"), vmem_limit_bytes=VMEM_LIMIT),
    )(sinks, q, kv, kv, kv, kv, band_bias)


def _swa_band_bias(rel_bias):
    qi = jnp.arange(SWA_BLOCK)[:, None]
    kj = jnp.arange(2 * SWA_BLOCK)[None, :]
    dist = qi + SWA_BLOCK - kj
    in_band = (dist >= 0) & (dist < SWA_WINDOW)
    table = rel_bias.astype(jnp.float32).T
    return jnp.where(in_band, table[:, _t5_bucket(dist)], NEG)


def _pad_ffn_weights(w_up, conv_w, conv_b, w_down):
    pad = D_FF_PAD - D_FF
    halves = lambda a: jnp.concatenate(
        [jnp.pad(a[..., :D_FF], [(0, 0)] * (a.ndim - 1) + [(0, pad)]),
         jnp.pad(a[..., D_FF:], [(0, 0)] * (a.ndim - 1) + [(0, pad)])], axis=-1)
    return (halves(w_up).astype(jnp.bfloat16), halves(conv_w), halves(conv_b[None, :]),
            jnp.pad(w_down, ((0, pad), (0, 0))).astype(jnp.bfloat16))


def kernel(x, rel_bias, moba_w_qkv, moba_w_o, swa_w_kv, swa_w_q, swa_sinks, swa_w_o,
           ffn_w_up, ffn_conv_w, ffn_conv_b, ffn_w_down, ln_g, ln_b):
    bsz, s, d = x.shape
    bf16 = jnp.bfloat16
    bias_own, bias_prev, bias_far = _moba_bias_tables(rel_bias)
    band_bias = _swa_band_bias(rel_bias)

    h = x.reshape(bsz * s, d)
    kv = None
    for layer in range(DEPTH):
        if layer < N_A_LAYERS:
            qkv = _proj(h, moba_w_qkv[layer].astype(bf16)).reshape(bsz, s, 3 * ATT_WIDTH)
            attn = _moba_attention(qkv, bias_own, bias_prev, bias_far)
            w_o = moba_w_o[layer]
        else:
            j = layer - N_A_LAYERS
            if kv is None:
                kv = _proj(h, swa_w_kv.astype(bf16)).reshape(bsz, s, 2 * KV_WIDTH)
            q = _proj(h, swa_w_q[j].astype(bf16)).reshape(bsz, s, ATT_WIDTH)
            attn = _swa_attention(q, kv, band_bias, swa_sinks[j].astype(jnp.float32))
            w_o = swa_w_o[j]
        h = _proj_ln(attn.reshape(bsz * s, ATT_WIDTH), w_o.astype(bf16), h,
                     ln_g[layer, 0][None, :], ln_b[layer, 0][None, :])
        w_up, conv_w, conv_b, w_down = _pad_ffn_weights(
            ffn_w_up[layer], ffn_conv_w[layer], ffn_conv_b[layer], ffn_w_down[layer])
        h = _conv_ffn_ln(h, w_up, conv_w, conv_b, w_down,
                         ln_g[layer, 1][None, :], ln_b[layer, 1][None, :], s)
    return h.reshape(bsz, s, d)
```

```python
import functools
import math

import jax
import jax.numpy as jnp
from jax import lax
from jax.experimental import pallas as pl
from jax.experimental.pallas import tpu as pltpu

D_MODEL = 2048
N_HEADS = 16
HEAD_DIM = 128
N_KV_HEADS = 4
Q_GROUP = N_HEADS // N_KV_HEADS
ATT_WIDTH = N_HEADS * HEAD_DIM
KV_WIDTH = N_KV_HEADS * HEAD_DIM
MOBA_BLOCK = 256
MOBA_TOPK = 3
SWA_WINDOW = 128
SWA_BLOCK = 128
NUM_BUCKETS = 32
MAX_DISTANCE = 128
D_FF = 5504
CONV_WIDTH = 3
DEPTH = 4
N_A_LAYERS = DEPTH // 2
ALPHA = (2.0 * DEPTH) ** 0.25
LN_EPS = 1e-5
NEG = -1e30
BIG = 1e30
SCALE = HEAD_DIM ** -0.5
LOG2E = math.log2(math.e)

V7X_VMEM_BYTES = 64 * 1024 * 1024
VMEM_LIMIT = 56 * 1024 * 1024
SUBLANES = 8
FF_TILE = 512
D_FF_PAD = -(-D_FF // FF_TILE) * FF_TILE
FFN_ROWS = 512
PROJ_ROWS = 1024
PROJ_COLS = 1024
LN_ROWS = 512
MOBA_HEADS_PER_STEP = 4

_NT = (((1,), (1,)), ((), ()))


def _t5_bucket(dist):
    n = jnp.maximum(dist, 0)
    max_exact = NUM_BUCKETS // 2
    nf = jnp.maximum(n, 1).astype(jnp.float32)
    large = max_exact + (jnp.log(nf / max_exact) / math.log(MAX_DISTANCE / max_exact)
                         * (NUM_BUCKETS - max_exact)).astype(jnp.int32)
    large = jnp.minimum(large, NUM_BUCKETS - 1)
    return jnp.where(n < max_exact, n, large)


def _layer_norm(y, g, b):
    mu = jnp.mean(y, axis=-1, keepdims=True)
    yc = y - mu
    var = jnp.mean(yc * yc, axis=-1, keepdims=True)
    return yc * lax.rsqrt(var + LN_EPS) * g + b


def _proj_kernel(h_ref, w_ref, o_ref, xb_ref, *, scaled_tiles, scale):
    j = pl.program_id(1)

    @pl.when(j == 0)
    def _():
        xb_ref[...] = h_ref[...].astype(jnp.bfloat16)

    r = jnp.dot(xb_ref[...], w_ref[...], preferred_element_type=jnp.float32)
    if scaled_tiles:
        r = r * jnp.where(j < scaled_tiles, scale, 1.0)
    o_ref[...] = r.astype(o_ref.dtype)


def _proj(h, w, scaled_cols=0, scale=1.0):
    t, d = h.shape
    n = w.shape[1]
    tm, tn = PROJ_ROWS, min(PROJ_COLS, n)
    assert scaled_cols % tn == 0
    return pl.pallas_call(
        functools.partial(_proj_kernel, scaled_tiles=scaled_cols // tn, scale=scale),
        name="proj",
        grid=(t // tm, n // tn),
        in_specs=[pl.BlockSpec((tm, d), lambda i, j: (i, 0)),
                  pl.BlockSpec((d, tn), lambda i, j: (0, j))],
        out_specs=pl.BlockSpec((tm, tn), lambda i, j: (i, j)),
        out_shape=jax.ShapeDtypeStruct((t, n), jnp.bfloat16),
        scratch_shapes=[pltpu.VMEM((tm, d), jnp.bfloat16)],
        compiler_params=pltpu.CompilerParams(
            dimension_semantics=("parallel", "arbitrary"), vmem_limit_bytes=VMEM_LIMIT),
    )(h, w)


def _proj_ln_kernel(a_ref, w_ref, h_ref, g_ref, b_ref, o_ref):
    y = ALPHA * h_ref[...] + jnp.dot(a_ref[...], w_ref[...], preferred_element_type=jnp.float32)
    o_ref[...] = _layer_norm(y, g_ref[...], b_ref[...])


def _proj_ln(a, w, h, g, b):
    t, k = a.shape
    d = w.shape[1]
    tm = LN_ROWS
    return pl.pallas_call(
        _proj_ln_kernel,
        name="proj_ln",
        grid=(t // tm,),
        in_specs=[pl.BlockSpec((tm, k), lambda i: (i, 0)),
                  pl.BlockSpec((k, d), lambda i: (0, 0)),
                  pl.BlockSpec((tm, d), lambda i: (i, 0)),
                  pl.BlockSpec((1, d), lambda i: (0, 0)),
                  pl.BlockSpec((1, d), lambda i: (0, 0))],
        out_specs=pl.BlockSpec((tm, d), lambda i: (i, 0)),
        out_shape=jax.ShapeDtypeStruct((t, d), jnp.float32),
        compiler_params=pltpu.CompilerParams(
            dimension_semantics=("parallel",), vmem_limit_bytes=VMEM_LIMIT),
    )(a, w, h, g, b)


def _ffn_kernel(h_ref, halo_ref, wg_ref, wv_ref, cwg_ref, cwv_ref, cbg_ref, cbv_ref, wd_ref,
                g_ref, b_ref, o_ref, x_ref, acc_ref, *, seq_len):
    i = pl.program_id(0)
    f = pl.program_id(1)
    tm = h_ref.shape[0]

    @pl.when(f == 0)
    def _():
        at_start = (i * tm) % seq_len == 0
        halo = jnp.where(at_start, 0.0, halo_ref[...])
        x_ref[0:SUBLANES, :] = halo.astype(jnp.bfloat16)
        x_ref[SUBLANES:, :] = h_ref[...].astype(jnp.bfloat16)
        acc_ref[...] = jnp.zeros_like(acc_ref)

    x = x_ref[...]

    def up_conv(w_ref, cw_ref, cb_ref):
        u = jnp.dot(x, w_ref[...], preferred_element_type=jnp.float32)
        u1 = pltpu.roll(u, 1, 0)
        u2 = pltpu.roll(u, 2, 0)
        cw = cw_ref[...]
        c = (u2[SUBLANES:] * cw[0:1] + u1[SUBLANES:] * cw[1:2] + u[SUBLANES:] * cw[2:3]
             + cb_ref[...])
        return c

    gate = up_conv(wg_ref, cwg_ref, cbg_ref)
    val = up_conv(wv_ref, cwv_ref, cbv_ref)
    act = (jax.nn.gelu(gate) * val).astype(jnp.bfloat16)
    acc_ref[...] += jnp.dot(act, wd_ref[...], preferred_element_type=jnp.float32)

    @pl.when(f == pl.num_programs(1) - 1)
    def _():
        y = ALPHA * h_ref[...] + acc_ref[...]
        o_ref[...] = _layer_norm(y, g_ref[...], b_ref[...])


def _conv_ffn_ln(h, w_up, conv_w, conv_b, w_down, g, b, seq_len):
    t, d = h.shape
    tm, tf = FFN_ROWS, FF_TILE
    nf = D_FF_PAD // tf
    halo_blocks = tm // SUBLANES
    return pl.pallas_call(
        functools.partial(_ffn_kernel, seq_len=seq_len),
        name="conv_ffn_ln",
        grid=(t // tm, nf),
        in_specs=[pl.BlockSpec((tm, d), lambda i, f: (i, 0)),
                  pl.BlockSpec((SUBLANES, d), lambda i, f: (jnp.maximum(i * halo_blocks - 1, 0), 0)),
                  pl.BlockSpec((d, tf), lambda i, f: (0, f)),
                  pl.BlockSpec((d, tf), lambda i, f: (0, nf + f)),
                  pl.BlockSpec((CONV_WIDTH, tf), lambda i, f: (0, f)),
                  pl.BlockSpec((CONV_WIDTH, tf), lambda i, f: (0, nf + f)),
                  pl.BlockSpec((1, tf), lambda i, f: (0, f)),
                  pl.BlockSpec((1, tf), lambda i, f: (0, nf + f)),
                  pl.BlockSpec((tf, d), lambda i, f: (f, 0)),
                  pl.BlockSpec((1, d), lambda i, f: (0, 0)),
                  pl.BlockSpec((1, d), lambda i, f: (0, 0))],
        out_specs=pl.BlockSpec((tm, d), lambda i, f: (i, 0)),
        out_shape=jax.ShapeDtypeStruct((t, d), jnp.float32),
        scratch_shapes=[pltpu.VMEM((tm + SUBLANES, d), jnp.bfloat16),
                        pltpu.VMEM((tm, d), jnp.float32)],
        compiler_params=pltpu.CompilerParams(
            dimension_semantics=("parallel", "arbitrary"), vmem_limit_bytes=VMEM_LIMIT),
    )(h, h, w_up, w_up, conv_w, conv_w, conv_b, conv_b, w_down, g, b)


def _moba_kernel(q_ref, k_ref, v_ref, bias_ref, o_ref,
                 kmean_ref, vt_ref, sel_ref, m_ref, l_ref, acc_ref, sa_ref, sb_ref):
    c = pl.program_id(2)
    n_heads, n_blocks = kmean_ref.shape[0], kmean_ref.shape[1]
    blk = MOBA_BLOCK
    cols = lambda j: slice(j * HEAD_DIM, (j + 1) * HEAD_DIM)

    @pl.when(c == 0)
    def _():
        rows = lax.broadcasted_iota(jnp.int32, (HEAD_DIM, HEAD_DIM), 0)
        lanes = lax.broadcasted_iota(jnp.int32, (HEAD_DIM, HEAD_DIM), 1)
        eye = (rows == lanes).astype(jnp.bfloat16)

        def build(n, carry):
            keys = pl.ds(pl.multiple_of(n * blk, blk), blk)
            for j in range(n_heads):
                kb = k_ref[0, keys, cols(j)]
                vb = v_ref[0, keys, cols(j)]
                kmean_ref[j, pl.ds(n, 1), :] = (
                    jnp.sum(kb.astype(jnp.float32), axis=0, keepdims=True) * (1.0 / blk))
                vt_ref[j, n] = lax.dot_general(eye, vb, _NT,
                                               preferred_element_type=jnp.float32).astype(jnp.bfloat16)
            return carry

        lax.fori_loop(0, n_blocks, build, 0)

    qs = [q_ref[0, :, cols(j)] for j in range(n_heads)]

    for j in range(n_heads):
        gate = lax.dot_general(kmean_ref[j], qs[j].astype(jnp.float32), _NT,
                               precision=lax.Precision.HIGHEST, preferred_element_type=jnp.float32)
        row = lax.broadcasted_iota(jnp.int32, gate.shape, 0)
        past = row < c
        gate = jnp.where(past, gate, NEG)
        rank = jnp.zeros(gate.shape, jnp.int32)
        for m in range(n_blocks):
            gm = gate[m:m + 1, :]
            beats = (gm > gate) | ((gm == gate) & (row > m))
            rank = rank + beats.astype(jnp.int32)
        sel_ref[j] = (((rank < MOBA_TOPK) & past) | (row == c)).astype(jnp.float32)
        m_ref[j] = jnp.full(m_ref.shape[1:], NEG, jnp.float32)
        l_ref[j] = jnp.zeros(l_ref.shape[1:], jnp.float32)
        acc_ref[j] = jnp.zeros(acc_ref.shape[1:], jnp.float32)

    def score_matmuls(dst_ref, n):
        keys = pl.ds(pl.multiple_of(jnp.minimum(n, c) * blk, blk), blk)
        for j in range(n_heads):
            dst_ref[j] = lax.dot_general(k_ref[0, keys, cols(j)], qs[j], _NT,
                                         preferred_element_type=jnp.float32)

    def softmax_update(src_ref, n):
        live = n <= c
        nb_ = jnp.minimum(n, c)
        kind = jnp.clip(n - (c - 2), 0, 2)
        ps, alphas = [], []
        for j in range(n_heads):
            s = src_ref[j] + bias_ref[j, kind]
            chosen = jnp.logical_and(sel_ref[j, pl.ds(nb_, 1), :] > 0.5, live)
            m_old = m_ref[j]
            m_new = jnp.where(chosen, jnp.maximum(m_old, jnp.max(s, axis=0, keepdims=True)), m_old)
            p = jnp.exp2(s - jnp.where(chosen, m_new, BIG))
            alphas.append(jnp.exp2(m_old - m_new))
            m_ref[j] = m_new
            l_ref[j] = alphas[j] * l_ref[j] + jnp.sum(p, axis=0, keepdims=True)
            ps.append(p.astype(jnp.bfloat16))
        for j in range(n_heads):
            acc_ref[j] = alphas[j] * acc_ref[j] + jnp.dot(
                vt_ref[j, nb_], ps[j], preferred_element_type=jnp.float32)

    score_matmuls(sa_ref, 0)

    def body(i, carry):
        n = 2 * i
        score_matmuls(sb_ref, n + 1)
        softmax_update(sa_ref, n)
        score_matmuls(sa_ref, n + 2)
        softmax_update(sb_ref, n + 1)
        return carry

    lax.fori_loop(0, (c + 2) // 2, body, 0)

    for j in range(n_heads):
        out_t = acc_ref[j] / l_ref[j]
        o_ref[0, :, cols(j)] = out_t.T.astype(o_ref.dtype)


def _moba_attention(qkv, bias_tiles):
    bsz, s, _ = qkv.shape
    nb = s // MOBA_BLOCK
    hpb = MOBA_HEADS_PER_STEP
    ng = N_HEADS // hpb
    gw = hpb * HEAD_DIM
    return pl.pallas_call(
        _moba_kernel,
        name="moba_attention",
        grid=(bsz, ng, nb),
        in_specs=[pl.BlockSpec((1, MOBA_BLOCK, gw), lambda b, g, c: (b, c, g)),
                  pl.BlockSpec((1, s, gw), lambda b, g, c: (b, 0, ng + g)),
                  pl.BlockSpec((1, s, gw), lambda b, g, c: (b, 0, 2 * ng + g)),
                  pl.BlockSpec((hpb, 3, MOBA_BLOCK, MOBA_BLOCK), lambda b, g, c: (g, 0, 0, 0))],
        out_specs=pl.BlockSpec((1, MOBA_BLOCK, gw), lambda b, g, c: (b, c, g)),
        out_shape=jax.ShapeDtypeStruct((bsz, s, ATT_WIDTH), jnp.bfloat16),
        scratch_shapes=[pltpu.VMEM((hpb, nb, HEAD_DIM), jnp.float32),
                        pltpu.VMEM((hpb, nb, HEAD_DIM, MOBA_BLOCK), jnp.bfloat16),
                        pltpu.VMEM((hpb, nb, MOBA_BLOCK), jnp.float32),
                        pltpu.VMEM((hpb, 1, MOBA_BLOCK), jnp.float32),
                        pltpu.VMEM((hpb, 1, MOBA_BLOCK), jnp.float32),
                        pltpu.VMEM((hpb, HEAD_DIM, MOBA_BLOCK), jnp.float32),
                        pltpu.VMEM((hpb, MOBA_BLOCK, MOBA_BLOCK), jnp.float32),
                        pltpu.VMEM((hpb, MOBA_BLOCK, MOBA_BLOCK), jnp.float32)],
        compiler_params=pltpu.CompilerParams(
            dimension_semantics=("parallel", "parallel", "arbitrary"), vmem_limit_bytes=VMEM_LIMIT),
    )(qkv, qkv, qkv, bias_tiles)


def _bias_by_bucket(rel_bias, bucket):
    onehot = (bucket[None] == jnp.arange(NUM_BUCKETS).reshape((NUM_BUCKETS,) + (1,) * bucket.ndim))
    return jnp.einsum('kh,k...->h...', rel_bias.astype(jnp.float32), onehot.astype(jnp.float32),
                      precision=lax.Precision.HIGHEST)


def _moba_bias_tables(rel_bias):
    key = jnp.arange(MOBA_BLOCK)[:, None]
    qry = jnp.arange(MOBA_BLOCK)[None, :]
    d_own = qry - key
    own = jnp.where(d_own >= 0, _bias_by_bucket(rel_bias, _t5_bucket(d_own)), NEG)
    prev = _bias_by_bucket(rel_bias, _t5_bucket(d_own + MOBA_BLOCK))
    far = jnp.broadcast_to(rel_bias.astype(jnp.float32)[NUM_BUCKETS - 1][:, None, None], own.shape)
    return jnp.stack([far, prev, own], axis=1) * LOG2E


def _swa_kernel(sink_ref, q_ref, kp_ref, kc_ref, vp_ref, vc_ref, bias_ref, o_ref):
    g = pl.program_id(1)
    c = pl.program_id(2)
    kp, kc, vp, vc = kp_ref[0], kc_ref[0], vp_ref[0], vc_ref[0]
    has_prev = c > 0
    for gi in range(Q_GROUP):
        q = q_ref[0, :, gi * HEAD_DIM:(gi + 1) * HEAD_DIM]
        bias = bias_ref[gi]
        s_prev = lax.dot_general(q, kp, _NT, preferred_element_type=jnp.float32) * SCALE \
            + bias[:, :SWA_BLOCK]
        s_prev = jnp.where(has_prev, s_prev, NEG)
        s_cur = lax.dot_general(q, kc, _NT, preferred_element_type=jnp.float32) * SCALE \
            + bias[:, SWA_BLOCK:]
        sink = sink_ref[g * Q_GROUP + gi]
        m = jnp.maximum(jnp.maximum(jnp.max(s_prev, axis=1, keepdims=True),
                                    jnp.max(s_cur, axis=1, keepdims=True)), sink)
        p_prev = jnp.exp(s_prev - m)
        p_cur = jnp.exp(s_cur - m)
        denom = (jnp.sum(p_prev, axis=1, keepdims=True) + jnp.sum(p_cur, axis=1, keepdims=True)
                 + jnp.exp(sink - m))
        o = (jnp.dot(p_prev.astype(jnp.bfloat16), vp, preferred_element_type=jnp.float32)
             + jnp.dot(p_cur.astype(jnp.bfloat16), vc, preferred_element_type=jnp.float32))
        o_ref[0, :, gi * HEAD_DIM:(gi + 1) * HEAD_DIM] = (o / denom).astype(o_ref.dtype)


def _swa_attention(q, kv, band_bias, sinks):
    bsz, s, _ = q.shape
    nq = s // SWA_BLOCK
    gw = Q_GROUP * HEAD_DIM
    prev = lambda c: jnp.maximum(c - 1, 0)
    return pl.pallas_call(
        _swa_kernel,
        name="swa_attention",
        grid=(bsz, N_KV_HEADS, nq),
        in_specs=[pl.BlockSpec(memory_space=pltpu.SMEM),
                  pl.BlockSpec((1, SWA_BLOCK, gw), lambda b, g, c: (b, c, g)),
                  pl.BlockSpec((1, SWA_BLOCK, HEAD_DIM), lambda b, g, c: (b, prev(c), g)),
                  pl.BlockSpec((1, SWA_BLOCK, HEAD_DIM), lambda b, g, c: (b, c, g)),
                  pl.BlockSpec((1, SWA_BLOCK, HEAD_DIM), lambda b, g, c: (b, prev(c), N_KV_HEADS + g)),
                  pl.BlockSpec((1, SWA_BLOCK, HEAD_DIM), lambda b, g, c: (b, c, N_KV_HEADS + g)),
                  pl.BlockSpec((Q_GROUP, SWA_BLOCK, 2 * SWA_BLOCK), lambda b, g, c: (g, 0, 0))],
        out_specs=pl.BlockSpec((1, SWA_BLOCK, gw), lambda b, g, c: (b, c, g)),
        out_shape=jax.ShapeDtypeStruct((bsz, s, ATT_WIDTH), jnp.bfloat16),
        compiler_params=pltpu.CompilerParams(
            dimension_semantics=("parallel", "parallel", "arbitrary"), vmem_limit_bytes=VMEM_LIMIT),
    )(sinks, q, kv, kv, kv, kv, band_bias)


def _swa_band_bias(rel_bias):
    qi = jnp.arange(SWA_BLOCK)[:, None]
    kj = jnp.arange(2 * SWA_BLOCK)[None, :]
    dist = qi + SWA_BLOCK - kj
    in_band = (dist >= 0) & (dist < SWA_WINDOW)
    return jnp.where(in_band, _bias_by_bucket(rel_bias, _t5_bucket(dist)), NEG)


def _pad_ffn_weights(w_up, conv_w, conv_b, w_down):
    pad = D_FF_PAD - D_FF
    halves = lambda a: jnp.concatenate(
        [jnp.pad(a[..., :D_FF], [(0, 0)] * (a.ndim - 1) + [(0, pad)]),
         jnp.pad(a[..., D_FF:], [(0, 0)] * (a.ndim - 1) + [(0, pad)])], axis=-1)
    return (halves(w_up).astype(jnp.bfloat16), halves(conv_w), halves(conv_b[None, :]),
            jnp.pad(w_down, ((0, pad), (0, 0))).astype(jnp.bfloat16))


def kernel(x, rel_bias, moba_w_qkv, moba_w_o, swa_w_kv, swa_w_q, swa_sinks, swa_w_o,
           ffn_w_up, ffn_conv_w, ffn_conv_b, ffn_w_down, ln_g, ln_b):
    bsz, s, d = x.shape
    bf16 = jnp.bfloat16
    moba_bias = _moba_bias_tables(rel_bias)
    band_bias = _swa_band_bias(rel_bias)

    h = x.reshape(bsz * s, d)
    kv = None
    for layer in range(DEPTH):
        if layer < N_A_LAYERS:
            qkv = _proj(h, moba_w_qkv[layer].astype(bf16), scaled_cols=ATT_WIDTH,
                        scale=SCALE * LOG2E).reshape(bsz, s, 3 * ATT_WIDTH)
            attn = _moba_attention(qkv, moba_bias)
            w_o = moba_w_o[layer]
        else:
            j = layer - N_A_LAYERS
            if kv is None:
                kv = _proj(h, swa_w_kv.astype(bf16)).reshape(bsz, s, 2 * KV_WIDTH)
            q = _proj(h, swa_w_q[j].astype(bf16)).reshape(bsz, s, ATT_WIDTH)
            attn = _swa_attention(q, kv, band_bias, swa_sinks[j].astype(jnp.float32))
            w_o = swa_w_o[j]
        h = _proj_ln(attn.reshape(bsz * s, ATT_WIDTH), w_o.astype(bf16), h,
                     ln_g[layer, 0][None, :], ln_b[layer, 0][None, :])
        w_up, conv_w, conv_b, w_down = _pad_ffn_weights(
            ffn_w_up[layer], ffn_conv_w[layer], ffn_conv_b[layer], ffn_w_down[layer])
        h = _conv_ffn_ln(h, w_up, conv_w, conv_b, w_down,
                         ln_g[layer, 1][None, :], ln_b[layer, 1][None, :], s)
    return h.reshape(bsz, s, d)
```

```python
import functools
import math

import jax
import jax.numpy as jnp
from jax import lax
from jax.experimental import pallas as pl
from jax.experimental.pallas import tpu as pltpu

D_MODEL = 2048
N_HEADS = 16
HEAD_DIM = 128
N_KV_HEADS = 4
Q_GROUP = N_HEADS // N_KV_HEADS
ATT_WIDTH = N_HEADS * HEAD_DIM
KV_WIDTH = N_KV_HEADS * HEAD_DIM
MOBA_BLOCK = 256
MOBA_TOPK = 3
SWA_WINDOW = 128
SWA_BLOCK = 128
NUM_BUCKETS = 32
MAX_DISTANCE = 128
D_FF = 5504
CONV_WIDTH = 3
DEPTH = 4
N_A_LAYERS = DEPTH // 2
ALPHA = (2.0 * DEPTH) ** 0.25
LN_EPS = 1e-5
NEG = -1e30
BIG = 1e30
SCALE = HEAD_DIM ** -0.5
LOG2E = math.log2(math.e)

V7X_VMEM_BYTES = 64 * 1024 * 1024
VMEM_LIMIT = 56 * 1024 * 1024
SUBLANES = 8
FF_TILE = 512
D_FF_PAD = -(-D_FF // FF_TILE) * FF_TILE
FFN_ROWS = 512
PROJ_ROWS = 1024
PROJ_COLS = 1024
LN_ROWS = 512
MOBA_HEADS_PER_STEP = 4

_NT = (((1,), (1,)), ((), ()))


def _t5_bucket(dist):
    n = jnp.maximum(dist, 0)
    max_exact = NUM_BUCKETS // 2
    nf = jnp.maximum(n, 1).astype(jnp.float32)
    large = max_exact + (jnp.log(nf / max_exact) / math.log(MAX_DISTANCE / max_exact)
                         * (NUM_BUCKETS - max_exact)).astype(jnp.int32)
    large = jnp.minimum(large, NUM_BUCKETS - 1)
    return jnp.where(n < max_exact, n, large)


def _layer_norm(y, g, b):
    mu = jnp.mean(y, axis=-1, keepdims=True)
    yc = y - mu
    var = jnp.mean(yc * yc, axis=-1, keepdims=True)
    return yc * lax.rsqrt(var + LN_EPS) * g + b


def _proj_kernel(h_ref, w_ref, o_ref, xb_ref, *, scaled_tiles, scale):
    j = pl.program_id(1)

    @pl.when(j == 0)
    def _():
        xb_ref[...] = h_ref[...].astype(jnp.bfloat16)

    r = jnp.dot(xb_ref[...], w_ref[...], preferred_element_type=jnp.float32)
    if scaled_tiles:
        r = r * jnp.where(j < scaled_tiles, scale, 1.0)
    o_ref[...] = r.astype(o_ref.dtype)


def _proj(h, w, layer, scaled_cols=0, scale=1.0):
    t, d = h.shape
    n = w.shape[2]
    tm, tn = PROJ_ROWS, min(PROJ_COLS, n)
    assert scaled_cols % tn == 0
    return pl.pallas_call(
        functools.partial(_proj_kernel, scaled_tiles=scaled_cols // tn, scale=scale),
        name="proj",
        grid=(t // tm, n // tn),
        in_specs=[pl.BlockSpec((tm, d), lambda i, j: (i, 0)),
                  pl.BlockSpec((None, d, tn), lambda i, j: (layer, 0, j))],
        out_specs=pl.BlockSpec((tm, tn), lambda i, j: (i, j)),
        out_shape=jax.ShapeDtypeStruct((t, n), jnp.bfloat16),
        scratch_shapes=[pltpu.VMEM((tm, d), jnp.bfloat16)],
        compiler_params=pltpu.CompilerParams(
            dimension_semantics=("parallel", "arbitrary"), vmem_limit_bytes=VMEM_LIMIT),
    )(h, w)


def _proj_ln_kernel(a_ref, w_ref, h_ref, g_ref, b_ref, o_ref):
    y = ALPHA * h_ref[...] + jnp.dot(a_ref[...], w_ref[...], preferred_element_type=jnp.float32)
    o_ref[...] = _layer_norm(y, g_ref[...], b_ref[...])


def _proj_ln(a, w, layer, h, g, b):
    t, k = a.shape
    d = w.shape[2]
    tm = LN_ROWS
    return pl.pallas_call(
        _proj_ln_kernel,
        name="proj_ln",
        grid=(t // tm,),
        in_specs=[pl.BlockSpec((tm, k), lambda i: (i, 0)),
                  pl.BlockSpec((None, k, d), lambda i: (layer, 0, 0)),
                  pl.BlockSpec((tm, d), lambda i: (i, 0)),
                  pl.BlockSpec((1, d), lambda i: (0, 0)),
                  pl.BlockSpec((1, d), lambda i: (0, 0))],
        out_specs=pl.BlockSpec((tm, d), lambda i: (i, 0)),
        out_shape=jax.ShapeDtypeStruct((t, d), jnp.float32),
        compiler_params=pltpu.CompilerParams(
            dimension_semantics=("parallel",), vmem_limit_bytes=VMEM_LIMIT),
    )(a, w, h, g, b)


def _ffn_kernel(h_ref, halo_ref, wg_ref, wv_ref, cwg_ref, cwv_ref, cbg_ref, cbv_ref, wd_ref,
                g_ref, b_ref, o_ref, x_ref, acc_ref, ua_ref, ub_ref, *, seq_len):
    i = pl.program_id(0)
    f = pl.program_id(1)
    tm = h_ref.shape[0]

    @pl.when(f == 0)
    def _():
        at_start = (i * tm) % seq_len == 0
        halo = jnp.where(at_start, 0.0, halo_ref[...])
        x_ref[0:SUBLANES, :] = halo.astype(jnp.bfloat16)
        x_ref[SUBLANES:, :] = h_ref[...].astype(jnp.bfloat16)
        acc_ref[...] = jnp.zeros_like(acc_ref)

    x = x_ref[...]
    tf = wg_ref.shape[1]
    halves = [slice(0, tf // 2), slice(tf // 2, tf)]

    base = jnp.minimum(f, 0)
    u_refs = (ua_ref, ub_ref)
    for u_ref, cs in zip(u_refs, halves):
        u_ref[base] = jnp.dot(x, wg_ref[:, cs], preferred_element_type=jnp.float32)
        u_ref[base + 1] = jnp.dot(x, wv_ref[:, cs], preferred_element_type=jnp.float32)

    def conv(u, cw_ref, cb_ref, cs):
        u1 = pltpu.roll(u, 1, 0)
        u2 = pltpu.roll(u, 2, 0)
        return (u2[SUBLANES:] * cw_ref[0:1, cs] + u1[SUBLANES:] * cw_ref[1:2, cs]
                + u[SUBLANES:] * cw_ref[2:3, cs] + cb_ref[:, cs])

    acts = []
    for u_ref, cs in zip(u_refs, halves):
        gate = conv(u_ref[0], cwg_ref, cbg_ref, cs)
        val = conv(u_ref[1], cwv_ref, cbv_ref, cs)
        acts.append((jax.nn.gelu(gate) * val).astype(jnp.bfloat16))
    for act, cs in zip(acts, halves):
        acc_ref[...] += jnp.dot(act, wd_ref[cs, :], preferred_element_type=jnp.float32)

    @pl.when(f == pl.num_programs(1) - 1)
    def _():
        y = ALPHA * h_ref[...] + acc_ref[...]
        o_ref[...] = _layer_norm(y, g_ref[...], b_ref[...])


def _conv_ffn_ln(h, w_up, conv_w, conv_b, w_down, layer, g, b, seq_len):
    t, d = h.shape
    tm, tf = FFN_ROWS, FF_TILE
    nf = D_FF_PAD // tf
    halo_blocks = tm // SUBLANES
    return pl.pallas_call(
        functools.partial(_ffn_kernel, seq_len=seq_len),
        name="conv_ffn_ln",
        grid=(t // tm, nf),
        in_specs=[pl.BlockSpec((tm, d), lambda i, f: (i, 0)),
                  pl.BlockSpec((SUBLANES, d), lambda i, f: (jnp.maximum(i * halo_blocks - 1, 0), 0)),
                  pl.BlockSpec((None, d, tf), lambda i, f: (layer, 0, f)),
                  pl.BlockSpec((None, d, tf), lambda i, f: (layer, 0, nf + f)),
                  pl.BlockSpec((None, CONV_WIDTH, tf), lambda i, f: (layer, 0, f)),
                  pl.BlockSpec((None, CONV_WIDTH, tf), lambda i, f: (layer, 0, nf + f)),
                  pl.BlockSpec((None, 1, tf), lambda i, f: (layer, 0, f)),
                  pl.BlockSpec((None, 1, tf), lambda i, f: (layer, 0, nf + f)),
                  pl.BlockSpec((None, tf, d), lambda i, f: (layer, f, 0)),
                  pl.BlockSpec((1, d), lambda i, f: (0, 0)),
                  pl.BlockSpec((1, d), lambda i, f: (0, 0))],
        out_specs=pl.BlockSpec((tm, d), lambda i, f: (i, 0)),
        out_shape=jax.ShapeDtypeStruct((t, d), jnp.float32),
        scratch_shapes=[pltpu.VMEM((tm + SUBLANES, d), jnp.bfloat16),
                        pltpu.VMEM((tm, d), jnp.float32),
                        pltpu.VMEM((2, tm + SUBLANES, tf // 2), jnp.float32),
                        pltpu.VMEM((2, tm + SUBLANES, tf // 2), jnp.float32)],
        compiler_params=pltpu.CompilerParams(
            dimension_semantics=("parallel", "arbitrary"), vmem_limit_bytes=VMEM_LIMIT),
    )(h, h, w_up, w_up, conv_w, conv_w, conv_b, conv_b, w_down, g, b)


def _moba_kernel(q_ref, k_ref, v_ref, bias_ref, o_ref,
                 kmean_ref, kparts_ref, vt_ref, sel_ref, m_ref, l_ref, acc_ref, sa_ref, sb_ref):
    c = pl.program_id(2)
    n_heads, n_blocks = kmean_ref.shape[0], kmean_ref.shape[1]
    blk = MOBA_BLOCK
    cols = lambda j: slice(j * HEAD_DIM, (j + 1) * HEAD_DIM)

    @pl.when(c == 0)
    def _():
        rows = lax.broadcasted_iota(jnp.int32, (HEAD_DIM, HEAD_DIM), 0)
        lanes = lax.broadcasted_iota(jnp.int32, (HEAD_DIM, HEAD_DIM), 1)
        eye = (rows == lanes).astype(jnp.bfloat16)

        def build(n, carry):
            keys = pl.ds(pl.multiple_of(n * blk, blk), blk)
            for j in range(n_heads):
                kb = k_ref[0, keys, cols(j)]
                vb = v_ref[0, keys, cols(j)]
                kmean_ref[j, pl.ds(n, 1), :] = (
                    jnp.sum(kb.astype(jnp.float32), axis=0, keepdims=True) * (1.0 / blk))
                vt_ref[j, n] = lax.dot_general(eye, vb, _NT,
                                               preferred_element_type=jnp.float32).astype(jnp.bfloat16)
            return carry

        lax.fori_loop(0, n_blocks, build, 0)
        for j in range(n_heads):
            km = kmean_ref[j]
            hi = km.astype(jnp.bfloat16)
            r1 = km - hi.astype(jnp.float32)
            mid = r1.astype(jnp.bfloat16)
            lo = (r1 - mid.astype(jnp.float32)).astype(jnp.bfloat16)
            kparts_ref[j] = jnp.concatenate([hi, mid, lo], axis=0)

    qs = [q_ref[0, :, cols(j)] for j in range(n_heads)]

    for j in range(n_heads):
        g3 = lax.dot_general(kparts_ref[j], qs[j], _NT, preferred_element_type=jnp.float32)
        gate = (g3[2 * n_blocks:] + g3[n_blocks:2 * n_blocks]) + g3[:n_blocks]
        row = lax.broadcasted_iota(jnp.int32, gate.shape, 0)
        past = row < c
        gate = jnp.where(past, gate, NEG)
        rank = jnp.zeros(gate.shape, jnp.int32)
        for m in range(n_blocks):
            gm = gate[m:m + 1, :]
            beats = (gm > gate) | ((gm == gate) & (row > m))
            rank = rank + beats.astype(jnp.int32)
        sel_ref[j] = (((rank < MOBA_TOPK) & past) | (row == c)).astype(jnp.float32)
        m_ref[j] = jnp.full(m_ref.shape[1:], NEG, jnp.float32)
        l_ref[j] = jnp.zeros(l_ref.shape[1:], jnp.float32)
        acc_ref[j] = jnp.zeros(acc_ref.shape[1:], jnp.float32)

    def score_matmuls(dst_ref, n):
        keys = pl.ds(pl.multiple_of(jnp.minimum(n, c) * blk, blk), blk)
        for j in range(n_heads):
            dst_ref[j] = lax.dot_general(k_ref[0, keys, cols(j)], qs[j], _NT,
                                         preferred_element_type=jnp.float32)

    def softmax_update(src_ref, n):
        live = n <= c
        nb_ = jnp.minimum(n, c)
        kind = jnp.clip(n - (c - 2), 0, 2)
        ps, alphas = [], []
        for j in range(n_heads):
            s = src_ref[j] + bias_ref[j, kind]
            chosen = jnp.logical_and(sel_ref[j, pl.ds(nb_, 1), :] > 0.5, live)
            m_old = m_ref[j]
            m_new = jnp.where(chosen, jnp.maximum(m_old, jnp.max(s, axis=0, keepdims=True)), m_old)
            p = jnp.exp2(s - jnp.where(chosen, m_new, BIG))
            alphas.append(jnp.exp2(m_old - m_new))
            m_ref[j] = m_new
            l_ref[j] = alphas[j] * l_ref[j] + jnp.sum(p, axis=0, keepdims=True)
            ps.append(p.astype(jnp.bfloat16))
        for j in range(n_heads):
            acc_ref[j] = alphas[j] * acc_ref[j] + jnp.dot(
                vt_ref[j, nb_], ps[j], preferred_element_type=jnp.float32)

    score_matmuls(sa_ref, 0)

    def body(i, carry):
        n = 2 * i
        score_matmuls(sb_ref, n + 1)
        softmax_update(sa_ref, n)
        score_matmuls(sa_ref, n + 2)
        softmax_update(sb_ref, n + 1)
        return carry

    lax.fori_loop(0, (c + 2) // 2, body, 0)

    for j in range(n_heads):
        out_t = acc_ref[j] / l_ref[j]
        o_ref[0, :, cols(j)] = out_t.T.astype(o_ref.dtype)


def _moba_attention(qkv, bias_tiles):
    bsz, s, _ = qkv.shape
    nb = s // MOBA_BLOCK
    hpb = MOBA_HEADS_PER_STEP
    ng = N_HEADS // hpb
    gw = hpb * HEAD_DIM
    return pl.pallas_call(
        _moba_kernel,
        name="moba_attention",
        grid=(bsz, ng, nb),
        in_specs=[pl.BlockSpec((1, MOBA_BLOCK, gw), lambda b, g, c: (b, c, g)),
                  pl.BlockSpec((1, s, gw), lambda b, g, c: (b, 0, ng + g)),
                  pl.BlockSpec((1, s, gw), lambda b, g, c: (b, 0, 2 * ng + g)),
                  pl.BlockSpec((hpb, 3, MOBA_BLOCK, MOBA_BLOCK), lambda b, g, c: (g, 0, 0, 0))],
        out_specs=pl.BlockSpec((1, MOBA_BLOCK, gw), lambda b, g, c: (b, c, g)),
        out_shape=jax.ShapeDtypeStruct((bsz, s, ATT_WIDTH), jnp.bfloat16),
        scratch_shapes=[pltpu.VMEM((hpb, nb, HEAD_DIM), jnp.float32),
                        pltpu.VMEM((hpb, 3 * nb, HEAD_DIM), jnp.bfloat16),
                        pltpu.VMEM((hpb, nb, HEAD_DIM, MOBA_BLOCK), jnp.bfloat16),
                        pltpu.VMEM((hpb, nb, MOBA_BLOCK), jnp.float32),
                        pltpu.VMEM((hpb, 1, MOBA_BLOCK), jnp.float32),
                        pltpu.VMEM((hpb, 1, MOBA_BLOCK), jnp.float32),
                        pltpu.VMEM((hpb, HEAD_DIM, MOBA_BLOCK), jnp.float32),
                        pltpu.VMEM((hpb, MOBA_BLOCK, MOBA_BLOCK), jnp.float32),
                        pltpu.VMEM((hpb, MOBA_BLOCK, MOBA_BLOCK), jnp.float32)],
        compiler_params=pltpu.CompilerParams(
            dimension_semantics=("parallel", "parallel", "arbitrary"), vmem_limit_bytes=VMEM_LIMIT),
    )(qkv, qkv, qkv, bias_tiles)


def _bias_by_bucket(rel_bias, bucket):
    onehot = (bucket[None] == jnp.arange(NUM_BUCKETS).reshape((NUM_BUCKETS,) + (1,) * bucket.ndim))
    return jnp.einsum('kh,k...->h...', rel_bias.astype(jnp.float32), onehot.astype(jnp.float32),
                      precision=lax.Precision.HIGHEST)


def _moba_bias_tables(rel_bias):
    key = jnp.arange(MOBA_BLOCK)[:, None]
    qry = jnp.arange(MOBA_BLOCK)[None, :]
    d_own = qry - key
    own = jnp.where(d_own >= 0, _bias_by_bucket(rel_bias, _t5_bucket(d_own)), NEG)
    prev = _bias_by_bucket(rel_bias, _t5_bucket(d_own + MOBA_BLOCK))
    far = jnp.broadcast_to(rel_bias.astype(jnp.float32)[NUM_BUCKETS - 1][:, None, None], own.shape)
    return jnp.stack([far, prev, own], axis=1) * LOG2E


def _swa_kernel(q_ref, kvp_ref, kvc_ref, bias_ref, sink_ref, o_ref):
    c = pl.program_id(1)
    has_prev = c > 0
    blk = SWA_BLOCK
    rows = lax.broadcasted_iota(jnp.int32, (HEAD_DIM, HEAD_DIM), 0)
    lanes = lax.broadcasted_iota(jnp.int32, (HEAD_DIM, HEAD_DIM), 1)
    eye = (rows == lanes).astype(jnp.bfloat16)
    head = lambda hh: slice(hh * HEAD_DIM, (hh + 1) * HEAD_DIM)
    kv_cols = lambda g, is_v: slice((is_v * N_KV_HEADS + g) * HEAD_DIM, (is_v * N_KV_HEADS + g + 1) * HEAD_DIM)

    scores, values = [], []
    for g in range(N_KV_HEADS):
        qg = jnp.concatenate([q_ref[0, :, head(g * Q_GROUP + i)] for i in range(Q_GROUP)], axis=0)
        scores.append(tuple(
            lax.dot_general(ref[0, :, kv_cols(g, 0)], qg, _NT, preferred_element_type=jnp.float32)
            for ref in (kvp_ref, kvc_ref)))
        values.append(tuple(
            lax.dot_general(eye, ref[0, :, kv_cols(g, 1)], _NT,
                            preferred_element_type=jnp.float32).astype(jnp.bfloat16)
            for ref in (kvp_ref, kvc_ref)))

    probs = []
    for g in range(N_KV_HEADS):
        s_prev = jnp.where(has_prev, scores[g][0] + bias_ref[g, :blk, :], NEG)
        s_cur = scores[g][1] + bias_ref[g, blk:, :]
        sink = sink_ref[g]
        m = jnp.maximum(jnp.maximum(jnp.max(s_prev, axis=0, keepdims=True),
                                    jnp.max(s_cur, axis=0, keepdims=True)), sink)
        p_prev = jnp.exp2(s_prev - m)
        p_cur = jnp.exp2(s_cur - m)
        denom = (jnp.sum(p_prev, axis=0, keepdims=True) + jnp.sum(p_cur, axis=0, keepdims=True)
                 + jnp.exp2(sink - m))
        probs.append((p_prev.astype(jnp.bfloat16), p_cur.astype(jnp.bfloat16), denom))

    for g in range(N_KV_HEADS):
        p_prev, p_cur, denom = probs[g]
        out_t = (jnp.dot(values[g][0], p_prev, preferred_element_type=jnp.float32)
                 + jnp.dot(values[g][1], p_cur, preferred_element_type=jnp.float32)) / denom
        for i in range(Q_GROUP):
            o_ref[0, :, head(g * Q_GROUP + i)] = out_t[:, i * blk:(i + 1) * blk].T.astype(o_ref.dtype)


def _swa_attention(q, kv, band_bias, sinks):
    bsz, s, _ = q.shape
    nq = s // SWA_BLOCK
    gq = Q_GROUP * SWA_BLOCK
    prev = lambda c: jnp.maximum(c - 1, 0)
    return pl.pallas_call(
        _swa_kernel,
        name="swa_attention",
        grid=(bsz, nq),
        in_specs=[pl.BlockSpec((1, SWA_BLOCK, ATT_WIDTH), lambda b, c: (b, c, 0)),
                  pl.BlockSpec((1, SWA_BLOCK, 2 * KV_WIDTH), lambda b, c: (b, prev(c), 0)),
                  pl.BlockSpec((1, SWA_BLOCK, 2 * KV_WIDTH), lambda b, c: (b, c, 0)),
                  pl.BlockSpec((N_KV_HEADS, 2 * SWA_BLOCK, gq), lambda b, c: (0, 0, 0)),
                  pl.BlockSpec((N_KV_HEADS, 1, gq), lambda b, c: (0, 0, 0))],
        out_specs=pl.BlockSpec((1, SWA_BLOCK, ATT_WIDTH), lambda b, c: (b, c, 0)),
        out_shape=jax.ShapeDtypeStruct((bsz, s, ATT_WIDTH), jnp.bfloat16),
        compiler_params=pltpu.CompilerParams(
            dimension_semantics=("parallel", "arbitrary"), vmem_limit_bytes=VMEM_LIMIT),
    )(q, kv, kv, band_bias, sinks)


def _swa_band_bias(rel_bias):
    qi = jnp.arange(SWA_BLOCK)[:, None]
    kj = jnp.arange(2 * SWA_BLOCK)[None, :]
    dist = qi + SWA_BLOCK - kj
    in_band = (dist >= 0) & (dist < SWA_WINDOW)
    bias = jnp.where(in_band, _bias_by_bucket(rel_bias, _t5_bucket(dist)) * LOG2E, NEG)
    bias = bias.reshape(N_KV_HEADS, Q_GROUP, SWA_BLOCK, 2 * SWA_BLOCK)
    return bias.transpose(0, 3, 1, 2).reshape(N_KV_HEADS, 2 * SWA_BLOCK, Q_GROUP * SWA_BLOCK)


def _swa_sink_rows(sinks):
    rows = jnp.repeat(sinks.astype(jnp.float32) * LOG2E, SWA_BLOCK)
    return rows.reshape(N_KV_HEADS, 1, Q_GROUP * SWA_BLOCK)


def _pad_ffn_weights(w_up, conv_w, conv_b, w_down):
    pad = D_FF_PAD - D_FF

    def halves(a):
        a = a.reshape(a.shape[:-1] + (2, D_FF))
        a = jnp.pad(a, [(0, 0)] * (a.ndim - 1) + [(0, pad)])
        return a.reshape(a.shape[:-2] + (2 * D_FF_PAD,))

    return (halves(w_up.astype(jnp.bfloat16)), halves(conv_w), halves(conv_b[:, None, :]),
            jnp.pad(w_down.astype(jnp.bfloat16), ((0, 0), (0, pad), (0, 0))))


def kernel(x, rel_bias, moba_w_qkv, moba_w_o, swa_w_kv, swa_w_q, swa_sinks, swa_w_o,
           ffn_w_up, ffn_conv_w, ffn_conv_b, ffn_w_down, ln_g, ln_b):
    bsz, s, d = x.shape
    bf16 = jnp.bfloat16
    moba_bias = _moba_bias_tables(rel_bias)
    band_bias = _swa_band_bias(rel_bias)
    w_qkv, w_o_moba = moba_w_qkv.astype(bf16), moba_w_o.astype(bf16)
    w_kv, w_q, w_o_swa = swa_w_kv.astype(bf16)[None], swa_w_q.astype(bf16), swa_w_o.astype(bf16)
    w_up, conv_w, conv_b, w_down = _pad_ffn_weights(ffn_w_up, ffn_conv_w, ffn_conv_b, ffn_w_down)

    h = x.reshape(bsz * s, d)
    kv = None
    for layer in range(DEPTH):
        if layer < N_A_LAYERS:
            qkv = _proj(h, w_qkv, layer, scaled_cols=ATT_WIDTH,
                        scale=SCALE * LOG2E).reshape(bsz, s, 3 * ATT_WIDTH)
            attn = _moba_attention(qkv, moba_bias)
            w_o, j = w_o_moba, layer
        else:
            j = layer - N_A_LAYERS
            if kv is None:
                kv = _proj(h, w_kv, 0).reshape(bsz, s, 2 * KV_WIDTH)
            q = _proj(h, w_q, j, scaled_cols=ATT_WIDTH, scale=SCALE * LOG2E).reshape(bsz, s, ATT_WIDTH)
            attn = _swa_attention(q, kv, band_bias, _swa_sink_rows(swa_sinks[j]))
            w_o = w_o_swa
        h = _proj_ln(attn.reshape(bsz * s, ATT_WIDTH), w_o, j, h,
                     ln_g[layer, 0][None, :], ln_b[layer, 0][None, :])
        h = _conv_ffn_ln(h, w_up, conv_w, conv_b, w_down, layer,
                         ln_g[layer, 1][None, :], ln_b[layer, 1][None, :], s)
    return h.reshape(bsz, s, d)
```

```python
import functools
import math

import jax
import jax.numpy as jnp
from jax import lax
from jax.experimental import pallas as pl
from jax.experimental.pallas import tpu as pltpu

D_MODEL = 2048
N_HEADS = 16
HEAD_DIM = 128
N_KV_HEADS = 4
Q_GROUP = N_HEADS // N_KV_HEADS
ATT_WIDTH = N_HEADS * HEAD_DIM
KV_WIDTH = N_KV_HEADS * HEAD_DIM
MOBA_BLOCK = 256
MOBA_TOPK = 3
SWA_WINDOW = 128
SWA_BLOCK = 128
NUM_BUCKETS = 32
MAX_DISTANCE = 128
D_FF = 5504
CONV_WIDTH = 3
DEPTH = 4
N_A_LAYERS = DEPTH // 2
ALPHA = (2.0 * DEPTH) ** 0.25
LN_EPS = 1e-5
NEG = -1e30
BIG = 1e30
SCALE = HEAD_DIM ** -0.5
LOG2E = math.log2(math.e)

V7X_VMEM_BYTES = 64 * 1024 * 1024
VMEM_LIMIT = 56 * 1024 * 1024
SUBLANES = 8
FF_TILE = 512
D_FF_PAD = -(-D_FF // FF_TILE) * FF_TILE
FFN_ROWS = 512
PROJ_ROWS = 1024
PROJ_COLS = 1024
LN_ROWS = 512
MOBA_HEADS_PER_STEP = 4

_NT = (((1,), (1,)), ((), ()))


def _t5_bucket(dist):
    n = jnp.maximum(dist, 0)
    max_exact = NUM_BUCKETS // 2
    nf = jnp.maximum(n, 1).astype(jnp.float32)
    large = max_exact + (jnp.log(nf / max_exact) / math.log(MAX_DISTANCE / max_exact)
                         * (NUM_BUCKETS - max_exact)).astype(jnp.int32)
    large = jnp.minimum(large, NUM_BUCKETS - 1)
    return jnp.where(n < max_exact, n, large)


def _layer_norm(y, g, b):
    mu = jnp.mean(y, axis=-1, keepdims=True)
    yc = y - mu
    var = jnp.mean(yc * yc, axis=-1, keepdims=True)
    return yc * lax.rsqrt(var + LN_EPS) * g + b


def _proj_kernel(h_ref, w_ref, o_ref, xb_ref, *, scaled_tiles, scale):
    j = pl.program_id(1)

    @pl.when(j == 0)
    def _():
        xb_ref[...] = h_ref[...].astype(jnp.bfloat16)

    r = jnp.dot(xb_ref[...], w_ref[...], preferred_element_type=jnp.float32)
    if scaled_tiles:
        r = r * jnp.where(j < scaled_tiles, scale, 1.0)
    o_ref[...] = r.astype(o_ref.dtype)


def _proj(h, w, layer, scaled_cols=0, scale=1.0):
    t, d = h.shape
    n = w.shape[2]
    tm, tn = PROJ_ROWS, min(PROJ_COLS, n)
    assert scaled_cols % tn == 0
    return pl.pallas_call(
        functools.partial(_proj_kernel, scaled_tiles=scaled_cols // tn, scale=scale),
        name="proj",
        grid=(t // tm, n // tn),
        in_specs=[pl.BlockSpec((tm, d), lambda i, j: (i, 0)),
                  pl.BlockSpec((None, d, tn), lambda i, j: (layer, 0, j))],
        out_specs=pl.BlockSpec((tm, tn), lambda i, j: (i, j)),
        out_shape=jax.ShapeDtypeStruct((t, n), jnp.bfloat16),
        scratch_shapes=[pltpu.VMEM((tm, d), jnp.bfloat16)],
        compiler_params=pltpu.CompilerParams(
            dimension_semantics=("parallel", "arbitrary"), vmem_limit_bytes=VMEM_LIMIT),
    )(h, w)


def _proj_ln_kernel(a_ref, w_ref, h_ref, g_ref, b_ref, o_ref):
    half = a_ref.shape[0] // 2
    for rows in (slice(0, half), slice(half, 2 * half)):
        y = ALPHA * h_ref[rows, :] + jnp.dot(a_ref[rows, :], w_ref[...], preferred_element_type=jnp.float32)
        o_ref[rows, :] = _layer_norm(y, g_ref[...], b_ref[...])


def _proj_ln(a, w, layer, h, g, b):
    t, k = a.shape
    d = w.shape[2]
    tm = LN_ROWS
    return pl.pallas_call(
        _proj_ln_kernel,
        name="proj_ln",
        grid=(t // tm,),
        in_specs=[pl.BlockSpec((tm, k), lambda i: (i, 0)),
                  pl.BlockSpec((None, k, d), lambda i: (layer, 0, 0)),
                  pl.BlockSpec((tm, d), lambda i: (i, 0)),
                  pl.BlockSpec((1, d), lambda i: (0, 0)),
                  pl.BlockSpec((1, d), lambda i: (0, 0))],
        out_specs=pl.BlockSpec((tm, d), lambda i: (i, 0)),
        out_shape=jax.ShapeDtypeStruct((t, d), jnp.float32),
        compiler_params=pltpu.CompilerParams(
            dimension_semantics=("parallel",), vmem_limit_bytes=VMEM_LIMIT),
    )(a, w, h, g, b)


def _ffn_kernel(h_ref, halo_ref, wg_ref, wv_ref, cwg_ref, cwv_ref, cbg_ref, cbv_ref, wd_ref,
                g_ref, b_ref, o_ref, x_ref, acc_ref, ua_ref, ub_ref, *, seq_len):
    i = pl.program_id(0)
    f = pl.program_id(1)
    tm = h_ref.shape[0]

    @pl.when(f == 0)
    def _():
        at_start = (i * tm) % seq_len == 0
        halo = jnp.where(at_start, 0.0, halo_ref[...])
        x_ref[0:SUBLANES, :] = halo.astype(jnp.bfloat16)
        x_ref[SUBLANES:, :] = h_ref[...].astype(jnp.bfloat16)
        acc_ref[...] = jnp.zeros_like(acc_ref)

    x = x_ref[...]
    tf = wg_ref.shape[1]
    halves = [slice(0, tf // 2), slice(tf // 2, tf)]

    base = jnp.minimum(f, 0)
    u_refs = (ua_ref, ub_ref)
    for u_ref, cs in zip(u_refs, halves):
        u_ref[base] = jnp.dot(x, wg_ref[:, cs], preferred_element_type=jnp.float32)
        u_ref[base + 1] = jnp.dot(x, wv_ref[:, cs], preferred_element_type=jnp.float32)

    def conv(u, cw_ref, cb_ref, cs):
        u1 = pltpu.roll(u, 1, 0)
        u2 = pltpu.roll(u, 2, 0)
        return (u2[SUBLANES:] * cw_ref[0:1, cs] + u1[SUBLANES:] * cw_ref[1:2, cs]
                + u[SUBLANES:] * cw_ref[2:3, cs] + cb_ref[:, cs])

    acts = []
    for u_ref, cs in zip(u_refs, halves):
        gate = conv(u_ref[0], cwg_ref, cbg_ref, cs)
        val = conv(u_ref[1], cwv_ref, cbv_ref, cs)
        acts.append((jax.nn.gelu(gate) * val).astype(jnp.bfloat16))
    for act, cs in zip(acts, halves):
        acc_ref[...] += jnp.dot(act, wd_ref[cs, :], preferred_element_type=jnp.float32)

    @pl.when(f == pl.num_programs(1) - 1)
    def _():
        y = ALPHA * h_ref[...] + acc_ref[...]
        o_ref[...] = _layer_norm(y, g_ref[...], b_ref[...])


def _conv_ffn_ln(h, ffn_weights, layer, g, b, seq_len):
    t, d = h.shape
    tm, tf = FFN_ROWS, FF_TILE
    nf = D_FF_PAD // tf
    halo_blocks = tm // SUBLANES
    return pl.pallas_call(
        functools.partial(_ffn_kernel, seq_len=seq_len),
        name="conv_ffn_ln",
        grid=(t // tm, nf),
        in_specs=[pl.BlockSpec((tm, d), lambda i, f: (i, 0)),
                  pl.BlockSpec((SUBLANES, d), lambda i, f: (jnp.maximum(i * halo_blocks - 1, 0), 0)),
                  pl.BlockSpec((None, d, tf), lambda i, f: (layer, 0, f)),
                  pl.BlockSpec((None, d, tf), lambda i, f: (layer, 0, f)),
                  pl.BlockSpec((None, CONV_WIDTH, tf), lambda i, f: (layer, 0, f)),
                  pl.BlockSpec((None, CONV_WIDTH, tf), lambda i, f: (layer, 0, f)),
                  pl.BlockSpec((None, 1, tf), lambda i, f: (layer, 0, f)),
                  pl.BlockSpec((None, 1, tf), lambda i, f: (layer, 0, f)),
                  pl.BlockSpec((None, tf, d), lambda i, f: (layer, f, 0)),
                  pl.BlockSpec((1, d), lambda i, f: (0, 0)),
                  pl.BlockSpec((1, d), lambda i, f: (0, 0))],
        out_specs=pl.BlockSpec((tm, d), lambda i, f: (i, 0)),
        out_shape=jax.ShapeDtypeStruct((t, d), jnp.float32),
        scratch_shapes=[pltpu.VMEM((tm + SUBLANES, d), jnp.bfloat16),
                        pltpu.VMEM((tm, d), jnp.float32),
                        pltpu.VMEM((2, tm + SUBLANES, tf // 2), jnp.float32),
                        pltpu.VMEM((2, tm + SUBLANES, tf // 2), jnp.float32)],
        compiler_params=pltpu.CompilerParams(
            dimension_semantics=("parallel", "arbitrary"), vmem_limit_bytes=VMEM_LIMIT),
    )(h, h, *ffn_weights, g, b)


def _moba_kernel(q_ref, k_ref, v_ref, bias_ref, o_ref,
                 kmean_ref, kparts_ref, vt_ref, sel_ref, m_ref, l_ref, acc_ref, sa_ref, sb_ref):
    c = pl.program_id(2)
    n_heads, n_blocks = kmean_ref.shape[0], kmean_ref.shape[1]
    blk = MOBA_BLOCK
    cols = lambda j: slice(j * HEAD_DIM, (j + 1) * HEAD_DIM)

    @pl.when(c == 0)
    def _():
        rows = lax.broadcasted_iota(jnp.int32, (HEAD_DIM, HEAD_DIM), 0)
        lanes = lax.broadcasted_iota(jnp.int32, (HEAD_DIM, HEAD_DIM), 1)
        eye = (rows == lanes).astype(jnp.bfloat16)

        def build(n, carry):
            keys = pl.ds(pl.multiple_of(n * blk, blk), blk)
            for j in range(n_heads):
                kb = k_ref[0, keys, cols(j)]
                vb = v_ref[0, keys, cols(j)]
                kmean_ref[j, pl.ds(n, 1), :] = (
                    jnp.sum(kb.astype(jnp.float32), axis=0, keepdims=True) * (1.0 / blk))
                vt_ref[j, n] = lax.dot_general(eye, vb, _NT,
                                               preferred_element_type=jnp.float32).astype(jnp.bfloat16)
            return carry

        lax.fori_loop(0, n_blocks, build, 0)
        for j in range(n_heads):
            km = kmean_ref[j]
            hi = km.astype(jnp.bfloat16)
            r1 = km - hi.astype(jnp.float32)
            mid = r1.astype(jnp.bfloat16)
            lo = (r1 - mid.astype(jnp.float32)).astype(jnp.bfloat16)
            kparts_ref[j] = jnp.concatenate([hi, mid, lo], axis=0)

    qs = [q_ref[0, :, cols(j)] for j in range(n_heads)]

    for j in range(n_heads):
        g3 = lax.dot_general(kparts_ref[j], qs[j], _NT, preferred_element_type=jnp.float32)
        gate = (g3[2 * n_blocks:] + g3[n_blocks:2 * n_blocks]) + g3[:n_blocks]
        row = lax.broadcasted_iota(jnp.int32, gate.shape, 0)
        past = row < c
        gate = jnp.where(past, gate, NEG)
        rank = jnp.zeros(gate.shape, jnp.int32)
        for m in range(n_blocks):
            gm = gate[m:m + 1, :]
            beats = (gm > gate) | ((gm == gate) & (row > m))
            rank = rank + beats.astype(jnp.int32)
        sel_ref[j] = (((rank < MOBA_TOPK) & past) | (row == c)).astype(jnp.float32)
        m_ref[j] = jnp.full(m_ref.shape[1:], NEG, jnp.float32)
        l_ref[j] = jnp.zeros(l_ref.shape[1:], jnp.float32)
        acc_ref[j] = jnp.zeros(acc_ref.shape[1:], jnp.float32)

    def score_matmuls(dst_ref, n):
        keys = pl.ds(pl.multiple_of(jnp.minimum(n, c) * blk, blk), blk)
        for j in range(n_heads):
            dst_ref[j] = lax.dot_general(k_ref[0, keys, cols(j)], qs[j], _NT,
                                         preferred_element_type=jnp.float32)

    def softmax_update(src_ref, n):
        live = n <= c
        nb_ = jnp.minimum(n, c)
        kind = jnp.clip(n - (c - 2), 0, 2)
        ps, alphas = [], []
        for j in range(n_heads):
            s = src_ref[j] + bias_ref[j, kind]
            chosen = jnp.logical_and(sel_ref[j, pl.ds(nb_, 1), :] > 0.5, live)
            m_old = m_ref[j]
            m_new = jnp.where(chosen, jnp.maximum(m_old, jnp.max(s, axis=0, keepdims=True)), m_old)
            p = jnp.exp2(s - jnp.where(chosen, m_new, BIG))
            alphas.append(jnp.exp2(m_old - m_new))
            m_ref[j] = m_new
            l_ref[j] = alphas[j] * l_ref[j] + jnp.sum(p, axis=0, keepdims=True)
            ps.append(p.astype(jnp.bfloat16))
        for j in range(n_heads):
            acc_ref[j] = alphas[j] * acc_ref[j] + jnp.dot(
                vt_ref[j, nb_], ps[j], preferred_element_type=jnp.float32)

    score_matmuls(sa_ref, 0)

    def body(i, carry):
        n = 2 * i
        score_matmuls(sb_ref, n + 1)
        softmax_update(sa_ref, n)
        score_matmuls(sa_ref, n + 2)
        softmax_update(sb_ref, n + 1)
        return carry

    lax.fori_loop(0, (c + 2) // 2, body, 0)

    for j in range(n_heads):
        out_t = acc_ref[j] / l_ref[j]
        o_ref[0, :, cols(j)] = out_t.T.astype(o_ref.dtype)


def _moba_attention(qkv, bias_tiles):
    bsz, s, _ = qkv.shape
    nb = s // MOBA_BLOCK
    hpb = MOBA_HEADS_PER_STEP
    ng = N_HEADS // hpb
    gw = hpb * HEAD_DIM
    return pl.pallas_call(
        _moba_kernel,
        name="moba_attention",
        grid=(bsz, ng, nb),
        in_specs=[pl.BlockSpec((1, MOBA_BLOCK, gw), lambda b, g, c: (b, c, g)),
                  pl.BlockSpec((1, s, gw), lambda b, g, c: (b, 0, ng + g)),
                  pl.BlockSpec((1, s, gw), lambda b, g, c: (b, 0, 2 * ng + g)),
                  pl.BlockSpec((hpb, 3, MOBA_BLOCK, MOBA_BLOCK), lambda b, g, c: (g, 0, 0, 0))],
        out_specs=pl.BlockSpec((1, MOBA_BLOCK, gw), lambda b, g, c: (b, c, g)),
        out_shape=jax.ShapeDtypeStruct((bsz, s, ATT_WIDTH), jnp.bfloat16),
        scratch_shapes=[pltpu.VMEM((hpb, nb, HEAD_DIM), jnp.float32),
                        pltpu.VMEM((hpb, 3 * nb, HEAD_DIM), jnp.bfloat16),
                        pltpu.VMEM((hpb, nb, HEAD_DIM, MOBA_BLOCK), jnp.bfloat16),
                        pltpu.VMEM((hpb, nb, MOBA_BLOCK), jnp.float32),
                        pltpu.VMEM((hpb, 1, MOBA_BLOCK), jnp.float32),
                        pltpu.VMEM((hpb, 1, MOBA_BLOCK), jnp.float32),
                        pltpu.VMEM((hpb, HEAD_DIM, MOBA_BLOCK), jnp.float32),
                        pltpu.VMEM((hpb, MOBA_BLOCK, MOBA_BLOCK), jnp.float32),
                        pltpu.VMEM((hpb, MOBA_BLOCK, MOBA_BLOCK), jnp.float32)],
        compiler_params=pltpu.CompilerParams(
            dimension_semantics=("parallel", "parallel", "arbitrary"), vmem_limit_bytes=VMEM_LIMIT),
    )(qkv, qkv, qkv, bias_tiles)


def _bias_by_bucket(rel_bias, bucket):
    onehot = (bucket[None] == jnp.arange(NUM_BUCKETS).reshape((NUM_BUCKETS,) + (1,) * bucket.ndim))
    return jnp.einsum('kh,k...->h...', rel_bias.astype(jnp.float32), onehot.astype(jnp.float32),
                      precision=lax.Precision.HIGHEST)


def _moba_bias_tables(rel_bias):
    key = jnp.arange(MOBA_BLOCK)[:, None]
    qry = jnp.arange(MOBA_BLOCK)[None, :]
    d_own = qry - key
    own = jnp.where(d_own >= 0, _bias_by_bucket(rel_bias, _t5_bucket(d_own)), NEG)
    prev = _bias_by_bucket(rel_bias, _t5_bucket(d_own + MOBA_BLOCK))
    far = jnp.broadcast_to(rel_bias.astype(jnp.float32)[NUM_BUCKETS - 1][:, None, None], own.shape)
    return jnp.stack([far, prev, own], axis=1) * LOG2E


def _swa_kernel(q_ref, kvp_ref, kvc_ref, bias_ref, sink_ref, o_ref):
    c = pl.program_id(1)
    has_prev = c > 0
    blk = SWA_BLOCK
    rows = lax.broadcasted_iota(jnp.int32, (HEAD_DIM, HEAD_DIM), 0)
    lanes = lax.broadcasted_iota(jnp.int32, (HEAD_DIM, HEAD_DIM), 1)
    eye = (rows == lanes).astype(jnp.bfloat16)
    head = lambda hh: slice(hh * HEAD_DIM, (hh + 1) * HEAD_DIM)
    kv_cols = lambda g, is_v: slice((is_v * N_KV_HEADS + g) * HEAD_DIM, (is_v * N_KV_HEADS + g + 1) * HEAD_DIM)

    scores, values = [], []
    for g in range(N_KV_HEADS):
        qg = jnp.concatenate([q_ref[0, :, head(g * Q_GROUP + i)] for i in range(Q_GROUP)], axis=0)
        scores.append(tuple(
            lax.dot_general(ref[0, :, kv_cols(g, 0)], qg, _NT, preferred_element_type=jnp.float32)
            for ref in (kvp_ref, kvc_ref)))
        values.append(tuple(
            lax.dot_general(eye, ref[0, :, kv_cols(g, 1)], _NT,
                            preferred_element_type=jnp.float32).astype(jnp.bfloat16)
            for ref in (kvp_ref, kvc_ref)))

    probs = []
    for g in range(N_KV_HEADS):
        s_prev = jnp.where(has_prev, scores[g][0] + bias_ref[g, :blk, :], NEG)
        s_cur = scores[g][1] + bias_ref[g, blk:, :]
        sink = sink_ref[g]
        m = jnp.maximum(jnp.maximum(jnp.max(s_prev, axis=0, keepdims=True),
                                    jnp.max(s_cur, axis=0, keepdims=True)), sink)
        p_prev = jnp.exp2(s_prev - m)
        p_cur = jnp.exp2(s_cur - m)
        denom = (jnp.sum(p_prev, axis=0, keepdims=True) + jnp.sum(p_cur, axis=0, keepdims=True)
                 + jnp.exp2(sink - m))
        probs.append((p_prev.astype(jnp.bfloat16), p_cur.astype(jnp.bfloat16), denom))

    for g in range(N_KV_HEADS):
        p_prev, p_cur, denom = probs[g]
        out_t = (jnp.dot(values[g][0], p_prev, preferred_element_type=jnp.float32)
                 + jnp.dot(values[g][1], p_cur, preferred_element_type=jnp.float32)) / denom
        for i in range(Q_GROUP):
            o_ref[0, :, head(g * Q_GROUP + i)] = out_t[:, i * blk:(i + 1) * blk].T.astype(o_ref.dtype)


def _swa_attention(q, kv, band_bias, sinks):
    bsz, s, _ = q.shape
    nq = s // SWA_BLOCK
    gq = Q_GROUP * SWA_BLOCK
    prev = lambda c: jnp.maximum(c - 1, 0)
    return pl.pallas_call(
        _swa_kernel,
        name="swa_attention",
        grid=(bsz, nq),
        in_specs=[pl.BlockSpec((1, SWA_BLOCK, ATT_WIDTH), lambda b, c: (b, c, 0)),
                  pl.BlockSpec((1, SWA_BLOCK, 2 * KV_WIDTH), lambda b, c: (b, prev(c), 0)),
                  pl.BlockSpec((1, SWA_BLOCK, 2 * KV_WIDTH), lambda b, c: (b, c, 0)),
                  pl.BlockSpec((N_KV_HEADS, 2 * SWA_BLOCK, gq), lambda b, c: (0, 0, 0)),
                  pl.BlockSpec((N_KV_HEADS, 1, gq), lambda b, c: (0, 0, 0))],
        out_specs=pl.BlockSpec((1, SWA_BLOCK, ATT_WIDTH), lambda b, c: (b, c, 0)),
        out_shape=jax.ShapeDtypeStruct((bsz, s, ATT_WIDTH), jnp.bfloat16),
        compiler_params=pltpu.CompilerParams(
            dimension_semantics=("parallel", "arbitrary"), vmem_limit_bytes=VMEM_LIMIT),
    )(q, kv, kv, band_bias, sinks)


def _swa_band_bias(rel_bias):
    qi = jnp.arange(SWA_BLOCK)[:, None]
    kj = jnp.arange(2 * SWA_BLOCK)[None, :]
    dist = qi + SWA_BLOCK - kj
    in_band = (dist >= 0) & (dist < SWA_WINDOW)
    bias = jnp.where(in_band, _bias_by_bucket(rel_bias, _t5_bucket(dist)) * LOG2E, NEG)
    bias = bias.reshape(N_KV_HEADS, Q_GROUP, SWA_BLOCK, 2 * SWA_BLOCK)
    return bias.transpose(0, 3, 1, 2).reshape(N_KV_HEADS, 2 * SWA_BLOCK, Q_GROUP * SWA_BLOCK)


def _swa_sink_rows(sinks):
    rows = jnp.repeat(sinks.astype(jnp.float32) * LOG2E, SWA_BLOCK)
    return rows.reshape(N_KV_HEADS, 1, Q_GROUP * SWA_BLOCK)


def _pad_ffn_weights(w_up, conv_w, conv_b, w_down):
    pad = D_FF_PAD - D_FF

    def halves(a):
        return tuple(jnp.pad(part, ((0, 0), (0, 0), (0, pad))) for part in (a[..., :D_FF], a[..., D_FF:]))

    return (halves(w_up.astype(jnp.bfloat16)) + halves(conv_w) + halves(conv_b[:, None, :])
            + (jnp.pad(w_down.astype(jnp.bfloat16), ((0, 0), (0, pad), (0, 0))),))


def kernel(x, rel_bias, moba_w_qkv, moba_w_o, swa_w_kv, swa_w_q, swa_sinks, swa_w_o,
           ffn_w_up, ffn_conv_w, ffn_conv_b, ffn_w_down, ln_g, ln_b):
    bsz, s, d = x.shape
    bf16 = jnp.bfloat16
    moba_bias = _moba_bias_tables(rel_bias)
    band_bias = _swa_band_bias(rel_bias)
    w_qkv, w_o_moba = moba_w_qkv.astype(bf16), moba_w_o.astype(bf16)
    w_kv, w_q, w_o_swa = swa_w_kv.astype(bf16)[None], swa_w_q.astype(bf16), swa_w_o.astype(bf16)
    ffn_weights = _pad_ffn_weights(ffn_w_up, ffn_conv_w, ffn_conv_b, ffn_w_down)

    h = x.reshape(bsz * s, d)
    kv = None
    for layer in range(DEPTH):
        if layer < N_A_LAYERS:
            qkv = _proj(h, w_qkv, layer, scaled_cols=ATT_WIDTH,
                        scale=SCALE * LOG2E).reshape(bsz, s, 3 * ATT_WIDTH)
            attn = _moba_attention(qkv, moba_bias)
            w_o, j = w_o_moba, layer
        else:
            j = layer - N_A_LAYERS
            if kv is None:
                kv = _proj(h, w_kv, 0).reshape(bsz, s, 2 * KV_WIDTH)
            q = _proj(h, w_q, j, scaled_cols=ATT_WIDTH, scale=SCALE * LOG2E).reshape(bsz, s, ATT_WIDTH)
            attn = _swa_attention(q, kv, band_bias, _swa_sink_rows(swa_sinks[j]))
            w_o = w_o_swa
        h = _proj_ln(attn.reshape(bsz * s, ATT_WIDTH), w_o, j, h,
                     ln_g[layer, 0][None, :], ln_b[layer, 0][None, :])
        h = _conv_ffn_ln(h, ffn_weights, layer,
                         ln_g[layer, 1][None, :], ln_b[layer, 1][None, :], s)
    return h.reshape(bsz, s, d)
```

```python
import functools
import math

import jax
import jax.numpy as jnp
from jax import lax
from jax.experimental import pallas as pl
from jax.experimental.pallas import tpu as pltpu

D_MODEL = 2048
N_HEADS = 16
HEAD_DIM = 128
N_KV_HEADS = 4
Q_GROUP = N_HEADS // N_KV_HEADS
ATT_WIDTH = N_HEADS * HEAD_DIM
KV_WIDTH = N_KV_HEADS * HEAD_DIM
MOBA_BLOCK = 256
MOBA_TOPK = 3
SWA_WINDOW = 128
SWA_BLOCK = 128
NUM_BUCKETS = 32
MAX_DISTANCE = 128
D_FF = 5504
CONV_WIDTH = 3
DEPTH = 4
N_A_LAYERS = DEPTH // 2
ALPHA = (2.0 * DEPTH) ** 0.25
LN_EPS = 1e-5
NEG = -1e30
BIG = 1e30
SCALE = HEAD_DIM ** -0.5
LOG2E = math.log2(math.e)

V7X_VMEM_BYTES = 64 * 1024 * 1024
VMEM_LIMIT = 56 * 1024 * 1024
SUBLANES = 8
FF_TILE = 512
D_FF_PAD = -(-D_FF // FF_TILE) * FF_TILE
FFN_ROWS = 512
PROJ_ROWS = 1024
PROJ_COLS = 1024
LN_ROWS = 512
MOBA_HEADS_PER_STEP = 4

_NT = (((1,), (1,)), ((), ()))


def _t5_bucket(dist):
    n = jnp.maximum(dist, 0)
    max_exact = NUM_BUCKETS // 2
    nf = jnp.maximum(n, 1).astype(jnp.float32)
    large = max_exact + (jnp.log(nf / max_exact) / math.log(MAX_DISTANCE / max_exact)
                         * (NUM_BUCKETS - max_exact)).astype(jnp.int32)
    large = jnp.minimum(large, NUM_BUCKETS - 1)
    return jnp.where(n < max_exact, n, large)


def _layer_norm(y, g, b):
    mu = jnp.mean(y, axis=-1, keepdims=True)
    yc = y - mu
    var = jnp.mean(yc * yc, axis=-1, keepdims=True)
    return yc * lax.rsqrt(var + LN_EPS) * g + b


def _proj_kernel(h_ref, w_ref, o_ref, xb_ref, *, scaled_tiles, scale):
    j = pl.program_id(1)

    @pl.when(j == 0)
    def _():
        xb_ref[...] = h_ref[...].astype(jnp.bfloat16)

    r = jnp.dot(xb_ref[...], w_ref[...], preferred_element_type=jnp.float32)
    if scaled_tiles:
        r = r * jnp.where(j < scaled_tiles, scale, 1.0)
    if len(o_ref.shape) == 3:
        gw = o_ref.shape[2]
        for k in range(o_ref.shape[0]):
            o_ref[k] = r[:, k * gw:(k + 1) * gw].astype(o_ref.dtype)
    else:
        o_ref[...] = r.astype(o_ref.dtype)


def _proj(h, w, layer, scaled_cols=0, scale=1.0, group=None):
    t, d = h.shape
    n = w.shape[2]
    tm, tn = PROJ_ROWS, min(PROJ_COLS, n)
    assert scaled_cols % tn == 0
    if group is None:
        out_spec = pl.BlockSpec((tm, tn), lambda i, j: (i, j))
        out_shape = jax.ShapeDtypeStruct((t, n), jnp.bfloat16)
    else:
        bsz, seq, gw = group
        tiles_per_seq = seq // tm
        assert bsz * seq == t and seq % tm == 0 and tn % gw == 0
        out_spec = pl.BlockSpec((None, tn // gw, tm, gw),
                                lambda i, j: (i // tiles_per_seq, j, i % tiles_per_seq, 0))
        out_shape = jax.ShapeDtypeStruct((bsz, n // gw, seq, gw), jnp.bfloat16)
    return pl.pallas_call(
        functools.partial(_proj_kernel, scaled_tiles=scaled_cols // tn, scale=scale),
        name="proj",
        grid=(t // tm, n // tn),
        in_specs=[pl.BlockSpec((tm, d), lambda i, j: (i, 0)),
                  pl.BlockSpec((None, d, tn), lambda i, j: (layer, 0, j))],
        out_specs=out_spec,
        out_shape=out_shape,
        scratch_shapes=[pltpu.VMEM((tm, d), jnp.bfloat16)],
        compiler_params=pltpu.CompilerParams(
            dimension_semantics=("parallel", "arbitrary"), vmem_limit_bytes=VMEM_LIMIT),
    )(h, w)


def _proj_ln_kernel(a_ref, w_ref, h_ref, g_ref, b_ref, o_ref):
    half = a_ref.shape[0] // 2
    for rows in (slice(0, half), slice(half, 2 * half)):
        y = ALPHA * h_ref[rows, :] + jnp.dot(a_ref[rows, :], w_ref[...], preferred_element_type=jnp.float32)
        o_ref[rows, :] = _layer_norm(y, g_ref[...], b_ref[...])


def _proj_ln(a, w, layer, h, g, b):
    t, k = a.shape
    d = w.shape[2]
    tm = LN_ROWS
    return pl.pallas_call(
        _proj_ln_kernel,
        name="proj_ln",
        grid=(t // tm,),
        in_specs=[pl.BlockSpec((tm, k), lambda i: (i, 0)),
                  pl.BlockSpec((None, k, d), lambda i: (layer, 0, 0)),
                  pl.BlockSpec((tm, d), lambda i: (i, 0)),
                  pl.BlockSpec((1, d), lambda i: (0, 0)),
                  pl.BlockSpec((1, d), lambda i: (0, 0))],
        out_specs=pl.BlockSpec((tm, d), lambda i: (i, 0)),
        out_shape=jax.ShapeDtypeStruct((t, d), jnp.float32),
        compiler_params=pltpu.CompilerParams(
            dimension_semantics=("parallel",), vmem_limit_bytes=VMEM_LIMIT),
    )(a, w, h, g, b)


def _ffn_kernel(h_ref, halo_ref, wg_ref, wv_ref, cwg_ref, cwv_ref, cbg_ref, cbv_ref, wd_ref,
                g_ref, b_ref, o_ref, x_ref, acc_ref, ua_ref, ub_ref, *, seq_len):
    i = pl.program_id(0)
    f = pl.program_id(1)
    tm = h_ref.shape[0]

    @pl.when(f == 0)
    def _():
        at_start = (i * tm) % seq_len == 0
        halo = jnp.where(at_start, 0.0, halo_ref[...])
        x_ref[0:SUBLANES, :] = halo.astype(jnp.bfloat16)
        x_ref[SUBLANES:, :] = h_ref[...].astype(jnp.bfloat16)
        acc_ref[...] = jnp.zeros_like(acc_ref)

    x = x_ref[...]
    tf = wg_ref.shape[1]
    halves = [slice(0, tf // 2), slice(tf // 2, tf)]

    base = jnp.minimum(f, 0)
    u_refs = (ua_ref, ub_ref)
    for u_ref, cs in zip(u_refs, halves):
        u_ref[base] = jnp.dot(x, wg_ref[:, cs], preferred_element_type=jnp.float32)
        u_ref[base + 1] = jnp.dot(x, wv_ref[:, cs], preferred_element_type=jnp.float32)

    def conv(u, cw_ref, cb_ref, cs):
        u1 = pltpu.roll(u, 1, 0)
        u2 = pltpu.roll(u, 2, 0)
        return (u2[SUBLANES:] * cw_ref[0:1, cs] + u1[SUBLANES:] * cw_ref[1:2, cs]
                + u[SUBLANES:] * cw_ref[2:3, cs] + cb_ref[:, cs])

    acts = []
    for u_ref, cs in zip(u_refs, halves):
        gate = conv(u_ref[0], cwg_ref, cbg_ref, cs)
        val = conv(u_ref[1], cwv_ref, cbv_ref, cs)
        acts.append((jax.nn.gelu(gate) * val).astype(jnp.bfloat16))
    for act, cs in zip(acts, halves):
        acc_ref[...] += jnp.dot(act, wd_ref[cs, :], preferred_element_type=jnp.float32)

    @pl.when(f == pl.num_programs(1) - 1)
    def _():
        y = ALPHA * h_ref[...] + acc_ref[...]
        o_ref[...] = _layer_norm(y, g_ref[...], b_ref[...])


def _conv_ffn_ln(h, ffn_weights, layer, g, b, seq_len):
    t, d = h.shape
    tm, tf = FFN_ROWS, FF_TILE
    nf = D_FF_PAD // tf
    halo_blocks = tm // SUBLANES
    return pl.pallas_call(
        functools.partial(_ffn_kernel, seq_len=seq_len),
        name="conv_ffn_ln",
        grid=(t // tm, nf),
        in_specs=[pl.BlockSpec((tm, d), lambda i, f: (i, 0)),
                  pl.BlockSpec((SUBLANES, d), lambda i, f: (jnp.maximum(i * halo_blocks - 1, 0), 0)),
                  pl.BlockSpec((None, d, tf), lambda i, f: (layer, 0, f)),
                  pl.BlockSpec((None, d, tf), lambda i, f: (layer, 0, f)),
                  pl.BlockSpec((None, CONV_WIDTH, tf), lambda i, f: (layer, 0, f)),
                  pl.BlockSpec((None, CONV_WIDTH, tf), lambda i, f: (layer, 0, f)),
                  pl.BlockSpec((None, 1, tf), lambda i, f: (layer, 0, f)),
                  pl.BlockSpec((None, 1, tf), lambda i, f: (layer, 0, f)),
                  pl.BlockSpec((None, tf, d), lambda i, f: (layer, f, 0)),
                  pl.BlockSpec((1, d), lambda i, f: (0, 0)),
                  pl.BlockSpec((1, d), lambda i, f: (0, 0))],
        out_specs=pl.BlockSpec((tm, d), lambda i, f: (i, 0)),
        out_shape=jax.ShapeDtypeStruct((t, d), jnp.float32),
        scratch_shapes=[pltpu.VMEM((tm + SUBLANES, d), jnp.bfloat16),
                        pltpu.VMEM((tm, d), jnp.float32),
                        pltpu.VMEM((2, tm + SUBLANES, tf // 2), jnp.float32),
                        pltpu.VMEM((2, tm + SUBLANES, tf // 2), jnp.float32)],
        compiler_params=pltpu.CompilerParams(
            dimension_semantics=("parallel", "arbitrary"), vmem_limit_bytes=VMEM_LIMIT),
    )(h, h, *ffn_weights, g, b)


def _moba_kernel(q_ref, k_ref, v_ref, bias_ref, o_ref,
                 kmean_ref, kparts_ref, vt_ref, sel_ref, m_ref, l_ref, acc_ref, sa_ref, sb_ref):
    c = pl.program_id(2)
    n_heads, n_blocks = kmean_ref.shape[0], kmean_ref.shape[1]
    blk = MOBA_BLOCK
    cols = lambda j: slice(j * HEAD_DIM, (j + 1) * HEAD_DIM)

    @pl.when(c == 0)
    def _():
        rows = lax.broadcasted_iota(jnp.int32, (HEAD_DIM, HEAD_DIM), 0)
        lanes = lax.broadcasted_iota(jnp.int32, (HEAD_DIM, HEAD_DIM), 1)
        eye = (rows == lanes).astype(jnp.bfloat16)

        def build(n, carry):
            keys = pl.ds(pl.multiple_of(n * blk, blk), blk)
            for j in range(n_heads):
                kb = k_ref[keys, cols(j)]
                vb = v_ref[keys, cols(j)]
                kmean_ref[j, pl.ds(n, 1), :] = (
                    jnp.sum(kb.astype(jnp.float32), axis=0, keepdims=True) * (1.0 / blk))
                vt_ref[j, n] = lax.dot_general(eye, vb, _NT,
                                               preferred_element_type=jnp.float32).astype(jnp.bfloat16)
            return carry

        lax.fori_loop(0, n_blocks, build, 0)
        for j in range(n_heads):
            km = kmean_ref[j]
            hi = km.astype(jnp.bfloat16)
            r1 = km - hi.astype(jnp.float32)
            mid = r1.astype(jnp.bfloat16)
            lo = (r1 - mid.astype(jnp.float32)).astype(jnp.bfloat16)
            kparts_ref[j] = jnp.concatenate([hi, mid, lo], axis=0)

    qs = [q_ref[:, cols(j)] for j in range(n_heads)]

    for j in range(n_heads):
        g3 = lax.dot_general(kparts_ref[j], qs[j], _NT, preferred_element_type=jnp.float32)
        gate = (g3[2 * n_blocks:] + g3[n_blocks:2 * n_blocks]) + g3[:n_blocks]
        row = lax.broadcasted_iota(jnp.int32, gate.shape, 0)
        past = row < c
        gate = jnp.where(past, gate, NEG)
        rank = jnp.zeros(gate.shape, jnp.int32)
        for m in range(n_blocks):
            gm = gate[m:m + 1, :]
            beats = (gm > gate) | ((gm == gate) & (row > m))
            rank = rank + beats.astype(jnp.int32)
        sel_ref[j] = (((rank < MOBA_TOPK) & past) | (row == c)).astype(jnp.float32)
        m_ref[j] = jnp.full(m_ref.shape[1:], NEG, jnp.float32)
        l_ref[j] = jnp.zeros(l_ref.shape[1:], jnp.float32)
        acc_ref[j] = jnp.zeros(acc_ref.shape[1:], jnp.float32)

    def score_matmuls(dst_ref, n):
        keys = pl.ds(pl.multiple_of(jnp.minimum(n, c) * blk, blk), blk)
        for j in range(n_heads):
            dst_ref[j] = lax.dot_general(k_ref[keys, cols(j)], qs[j], _NT,
                                         preferred_element_type=jnp.float32)

    def softmax_update(src_ref, n, far):
        live = n <= c
        nb_ = jnp.minimum(n, c)
        kind = jnp.clip(n - (c - 2), 0, 2)
        ps, alphas = [], []
        for j in range(n_heads):
            if far:
                s = src_ref[j]
                shift = bias_ref[j, 0, 0:1, :]
            else:
                s = src_ref[j] + bias_ref[j, kind]
                shift = 0.0
            chosen = jnp.logical_and(sel_ref[j, pl.ds(nb_, 1), :] > 0.5, live)
            m_old = m_ref[j]
            m_new = jnp.where(chosen, jnp.maximum(m_old, jnp.max(s, axis=0, keepdims=True) + shift), m_old)
            p = jnp.exp2(s - jnp.where(chosen, m_new - shift, BIG))
            alphas.append(jnp.exp2(m_old - m_new))
            m_ref[j] = m_new
            l_ref[j] = alphas[j] * l_ref[j] + jnp.sum(p, axis=0, keepdims=True)
            ps.append(p.astype(jnp.bfloat16))
        for j in range(n_heads):
            acc_ref[j] = alphas[j] * acc_ref[j] + jnp.dot(
                vt_ref[j, nb_], ps[j], preferred_element_type=jnp.float32)

    score_matmuls(sa_ref, 0)

    def pair(far):
        def body(i, carry):
            n = 2 * i
            score_matmuls(sb_ref, n + 1)
            softmax_update(sa_ref, n, far)
            score_matmuls(sa_ref, n + 2)
            softmax_update(sb_ref, n + 1, far)
            return carry
        return body

    far_pairs = jnp.maximum(c - 1, 0) // 2
    lax.fori_loop(0, far_pairs, pair(True), 0)
    lax.fori_loop(far_pairs, (c + 2) // 2, pair(False), 0)

    for j in range(n_heads):
        out_t = acc_ref[j] / l_ref[j]
        o_ref[0, :, cols(j)] = out_t.T.astype(o_ref.dtype)


def _moba_attention(qkv, bias_tiles):
    hpb = MOBA_HEADS_PER_STEP
    ng = N_HEADS // hpb
    gw = hpb * HEAD_DIM
    bsz, _, s, _ = qkv.shape
    assert qkv.shape == (bsz, 3 * ng, s, gw)
    nb = s // MOBA_BLOCK
    return pl.pallas_call(
        _moba_kernel,
        name="moba_attention",
        grid=(bsz, ng, nb),
        in_specs=[pl.BlockSpec((None, None, MOBA_BLOCK, gw), lambda b, g, c: (b, g, c, 0)),
                  pl.BlockSpec((None, None, s, gw), lambda b, g, c: (b, ng + g, 0, 0)),
                  pl.BlockSpec((None, None, s, gw), lambda b, g, c: (b, 2 * ng + g, 0, 0)),
                  pl.BlockSpec((hpb, 3, MOBA_BLOCK, MOBA_BLOCK), lambda b, g, c: (g, 0, 0, 0))],
        out_specs=pl.BlockSpec((1, MOBA_BLOCK, gw), lambda b, g, c: (b, c, g)),
        out_shape=jax.ShapeDtypeStruct((bsz, s, ATT_WIDTH), jnp.bfloat16),
        scratch_shapes=[pltpu.VMEM((hpb, nb, HEAD_DIM), jnp.float32),
                        pltpu.VMEM((hpb, 3 * nb, HEAD_DIM), jnp.bfloat16),
                        pltpu.VMEM((hpb, nb, HEAD_DIM, MOBA_BLOCK), jnp.bfloat16),
                        pltpu.VMEM((hpb, nb, MOBA_BLOCK), jnp.float32),
                        pltpu.VMEM((hpb, 1, MOBA_BLOCK), jnp.float32),
                        pltpu.VMEM((hpb, 1, MOBA_BLOCK), jnp.float32),
                        pltpu.VMEM((hpb, HEAD_DIM, MOBA_BLOCK), jnp.float32),
                        pltpu.VMEM((hpb, MOBA_BLOCK, MOBA_BLOCK), jnp.float32),
                        pltpu.VMEM((hpb, MOBA_BLOCK, MOBA_BLOCK), jnp.float32)],
        compiler_params=pltpu.CompilerParams(
            dimension_semantics=("parallel", "parallel", "arbitrary"), vmem_limit_bytes=VMEM_LIMIT),
    )(qkv, qkv, qkv, bias_tiles)


def _bias_by_bucket(rel_bias, bucket):
    onehot = (bucket[None] == jnp.arange(NUM_BUCKETS).reshape((NUM_BUCKETS,) + (1,) * bucket.ndim))
    return jnp.einsum('kh,k...->h...', rel_bias.astype(jnp.float32), onehot.astype(jnp.float32),
                      precision=lax.Precision.HIGHEST)


def _moba_bias_tables(rel_bias):
    key = jnp.arange(MOBA_BLOCK)[:, None]
    qry = jnp.arange(MOBA_BLOCK)[None, :]
    d_own = qry - key
    own = jnp.where(d_own >= 0, _bias_by_bucket(rel_bias, _t5_bucket(d_own)), NEG)
    prev = _bias_by_bucket(rel_bias, _t5_bucket(d_own + MOBA_BLOCK))
    far = jnp.broadcast_to(rel_bias.astype(jnp.float32)[NUM_BUCKETS - 1][:, None, None], own.shape)
    return jnp.stack([far, prev, own], axis=1) * LOG2E


def _swa_kernel(q_ref, kvp_ref, kvc_ref, bias_ref, sink_ref, o_ref):
    c = pl.program_id(1)
    has_prev = c > 0
    blk = SWA_BLOCK
    rows = lax.broadcasted_iota(jnp.int32, (HEAD_DIM, HEAD_DIM), 0)
    lanes = lax.broadcasted_iota(jnp.int32, (HEAD_DIM, HEAD_DIM), 1)
    eye = (rows == lanes).astype(jnp.bfloat16)
    head = lambda hh: slice(hh * HEAD_DIM, (hh + 1) * HEAD_DIM)
    kv_cols = lambda g, is_v: slice((is_v * N_KV_HEADS + g) * HEAD_DIM, (is_v * N_KV_HEADS + g + 1) * HEAD_DIM)

    scores, values = [], []
    for g in range(N_KV_HEADS):
        qg = jnp.concatenate([q_ref[0, :, head(g * Q_GROUP + i)] for i in range(Q_GROUP)], axis=0)
        scores.append(tuple(
            lax.dot_general(ref[0, :, kv_cols(g, 0)], qg, _NT, preferred_element_type=jnp.float32)
            for ref in (kvp_ref, kvc_ref)))
        values.append(tuple(
            lax.dot_general(eye, ref[0, :, kv_cols(g, 1)], _NT,
                            preferred_element_type=jnp.float32).astype(jnp.bfloat16)
            for ref in (kvp_ref, kvc_ref)))

    probs = []
    for g in range(N_KV_HEADS):
        s_prev = jnp.where(has_prev, scores[g][0] + bias_ref[g, :blk, :], NEG)
        s_cur = scores[g][1] + bias_ref[g, blk:, :]
        sink = sink_ref[g]
        m = jnp.maximum(jnp.maximum(jnp.max(s_prev, axis=0, keepdims=True),
                                    jnp.max(s_cur, axis=0, keepdims=True)), sink)
        p_prev = jnp.exp2(s_prev - m)
        p_cur = jnp.exp2(s_cur - m)
        denom = (jnp.sum(p_prev, axis=0, keepdims=True) + jnp.sum(p_cur, axis=0, keepdims=True)
                 + jnp.exp2(sink - m))
        probs.append((p_prev.astype(jnp.bfloat16), p_cur.astype(jnp.bfloat16), denom))

    for g in range(N_KV_HEADS):
        p_prev, p_cur, denom = probs[g]
        out_t = (jnp.dot(values[g][0], p_prev, preferred_element_type=jnp.float32)
                 + jnp.dot(values[g][1], p_cur, preferred_element_type=jnp.float32)) / denom
        for i in range(Q_GROUP):
            o_ref[0, :, head(g * Q_GROUP + i)] = out_t[:, i * blk:(i + 1) * blk].T.astype(o_ref.dtype)


def _swa_attention(q, kv, band_bias, sinks):
    bsz, s, _ = q.shape
    nq = s // SWA_BLOCK
    gq = Q_GROUP * SWA_BLOCK
    prev = lambda c: jnp.maximum(c - 1, 0)
    return pl.pallas_call(
        _swa_kernel,
        name="swa_attention",
        grid=(bsz, nq),
        in_specs=[pl.BlockSpec((1, SWA_BLOCK, ATT_WIDTH), lambda b, c: (b, c, 0)),
                  pl.BlockSpec((1, SWA_BLOCK, 2 * KV_WIDTH), lambda b, c: (b, prev(c), 0)),
                  pl.BlockSpec((1, SWA_BLOCK, 2 * KV_WIDTH), lambda b, c: (b, c, 0)),
                  pl.BlockSpec((N_KV_HEADS, 2 * SWA_BLOCK, gq), lambda b, c: (0, 0, 0)),
                  pl.BlockSpec((N_KV_HEADS, 1, gq), lambda b, c: (0, 0, 0))],
        out_specs=pl.BlockSpec((1, SWA_BLOCK, ATT_WIDTH), lambda b, c: (b, c, 0)),
        out_shape=jax.ShapeDtypeStruct((bsz, s, ATT_WIDTH), jnp.bfloat16),
        compiler_params=pltpu.CompilerParams(
            dimension_semantics=("parallel", "arbitrary"), vmem_limit_bytes=VMEM_LIMIT),
    )(q, kv, kv, band_bias, sinks)


def _swa_band_bias(rel_bias):
    qi = jnp.arange(SWA_BLOCK)[:, None]
    kj = jnp.arange(2 * SWA_BLOCK)[None, :]
    dist = qi + SWA_BLOCK - kj
    in_band = (dist >= 0) & (dist < SWA_WINDOW)
    bias = jnp.where(in_band, _bias_by_bucket(rel_bias, _t5_bucket(dist)) * LOG2E, NEG)
    bias = bias.reshape(N_KV_HEADS, Q_GROUP, SWA_BLOCK, 2 * SWA_BLOCK)
    return bias.transpose(0, 3, 1, 2).reshape(N_KV_HEADS, 2 * SWA_BLOCK, Q_GROUP * SWA_BLOCK)


def _swa_sink_rows(sinks):
    rows = jnp.repeat(sinks.astype(jnp.float32) * LOG2E, SWA_BLOCK)
    return rows.reshape(N_KV_HEADS, 1, Q_GROUP * SWA_BLOCK)


def _pad_ffn_weights(w_up, conv_w, conv_b, w_down):
    pad = D_FF_PAD - D_FF

    def halves(a, dtype):
        return tuple(jnp.pad(part.astype(dtype), ((0, 0), (0, 0), (0, pad)))
                     for part in (a[..., :D_FF], a[..., D_FF:]))

    return (halves(w_up, jnp.bfloat16) + halves(conv_w, jnp.float32) + halves(conv_b[:, None, :], jnp.float32)
            + (jnp.pad(w_down.astype(jnp.bfloat16), ((0, 0), (0, pad), (0, 0))),))


def kernel(x, rel_bias, moba_w_qkv, moba_w_o, swa_w_kv, swa_w_q, swa_sinks, swa_w_o,
           ffn_w_up, ffn_conv_w, ffn_conv_b, ffn_w_down, ln_g, ln_b):
    bsz, s, d = x.shape
    bf16 = jnp.bfloat16
    moba_bias = _moba_bias_tables(rel_bias)
    band_bias = _swa_band_bias(rel_bias)
    w_qkv, w_o_moba = moba_w_qkv.astype(bf16), moba_w_o.astype(bf16)
    w_kv, w_q, w_o_swa = swa_w_kv.astype(bf16)[None], swa_w_q.astype(bf16), swa_w_o.astype(bf16)
    ffn_weights = _pad_ffn_weights(ffn_w_up, ffn_conv_w, ffn_conv_b, ffn_w_down)

    h = x.reshape(bsz * s, d)
    kv = None
    for layer in range(DEPTH):
        if layer < N_A_LAYERS:
            qkv = _proj(h, w_qkv, layer, scaled_cols=ATT_WIDTH, scale=SCALE * LOG2E,
                        group=(bsz, s, MOBA_HEADS_PER_STEP * HEAD_DIM))
            attn = _moba_attention(qkv, moba_bias)
            w_o, j = w_o_moba, layer
        else:
            j = layer - N_A_LAYERS
            if kv is None:
                kv = _proj(h, w_kv, 0).reshape(bsz, s, 2 * KV_WIDTH)
            q = _proj(h, w_q, j, scaled_cols=ATT_WIDTH, scale=SCALE * LOG2E).reshape(bsz, s, ATT_WIDTH)
            attn = _swa_attention(q, kv, band_bias, _swa_sink_rows(swa_sinks[j]))
            w_o = w_o_swa
        h = _proj_ln(attn.reshape(bsz * s, ATT_WIDTH), w_o, j, h,
                     ln_g[layer, 0][None, :], ln_b[layer, 0][None, :])
        h = _conv_ffn_ln(h, ffn_weights, layer,
                         ln_g[layer, 1][None, :], ln_b[layer, 1][None, :], s)
    return h.reshape(bsz, s, d)
```

```python
import functools
import math

import jax
import jax.numpy as jnp
from jax import lax
from jax.experimental import pallas as pl
from jax.experimental.pallas import tpu as pltpu

D_MODEL = 2048
N_HEADS = 16
HEAD_DIM = 128
N_KV_HEADS = 4
Q_GROUP = N_HEADS // N_KV_HEADS
ATT_WIDTH = N_HEADS * HEAD_DIM
KV_WIDTH = N_KV_HEADS * HEAD_DIM
MOBA_BLOCK = 256
MOBA_TOPK = 3
SWA_WINDOW = 128
SWA_BLOCK = 128
NUM_BUCKETS = 32
MAX_DISTANCE = 128
D_FF = 5504
CONV_WIDTH = 3
DEPTH = 4
N_A_LAYERS = DEPTH // 2
ALPHA = (2.0 * DEPTH) ** 0.25
LN_EPS = 1e-5
NEG = -1e30
BIG = 1e30
SCALE = HEAD_DIM ** -0.5
LOG2E = math.log2(math.e)

V7X_VMEM_BYTES = 64 * 1024 * 1024
VMEM_LIMIT = 56 * 1024 * 1024
SUBLANES = 8
FF_TILE = 512
D_FF_PAD = -(-D_FF // FF_TILE) * FF_TILE
FFN_ROWS = 512
PROJ_ROWS = 1024
PROJ_COLS = 1024
LN_ROWS = 512
MOBA_HEADS_PER_STEP = 4

_NT = (((1,), (1,)), ((), ()))


def _t5_bucket(dist):
    n = jnp.maximum(dist, 0)
    max_exact = NUM_BUCKETS // 2
    nf = jnp.maximum(n, 1).astype(jnp.float32)
    large = max_exact + (jnp.log(nf / max_exact) / math.log(MAX_DISTANCE / max_exact)
                         * (NUM_BUCKETS - max_exact)).astype(jnp.int32)
    large = jnp.minimum(large, NUM_BUCKETS - 1)
    return jnp.where(n < max_exact, n, large)


def _layer_norm(y, g, b):
    mu = jnp.mean(y, axis=-1, keepdims=True)
    yc = y - mu
    var = jnp.mean(yc * yc, axis=-1, keepdims=True)
    return yc * lax.rsqrt(var + LN_EPS) * g + b


def _proj_kernel(h_ref, w_ref, o_ref, xb_ref, *, scaled_tiles, scale):
    j = pl.program_id(1)

    @pl.when(j == 0)
    def _():
        xb_ref[...] = h_ref[...].astype(jnp.bfloat16)

    r = jnp.dot(xb_ref[...], w_ref[...], preferred_element_type=jnp.float32)
    if scaled_tiles:
        r = r * jnp.where(j < scaled_tiles, scale, 1.0)
    if len(o_ref.shape) == 3:
        gw = o_ref.shape[2]
        for k in range(o_ref.shape[0]):
            o_ref[k] = r[:, k * gw:(k + 1) * gw].astype(o_ref.dtype)
    else:
        o_ref[...] = r.astype(o_ref.dtype)


def _proj(h, w, layer, scaled_cols=0, scale=1.0, group=None):
    t, d = h.shape
    n = w.shape[2]
    tm, tn = PROJ_ROWS, min(PROJ_COLS, n)
    assert scaled_cols % tn == 0
    if group is None:
        out_spec = pl.BlockSpec((tm, tn), lambda i, j: (i, j))
        out_shape = jax.ShapeDtypeStruct((t, n), jnp.bfloat16)
    else:
        bsz, seq, gw = group
        tiles_per_seq = seq // tm
        assert bsz * seq == t and seq % tm == 0 and tn % gw == 0
        out_spec = pl.BlockSpec((None, tn // gw, tm, gw),
                                lambda i, j: (i // tiles_per_seq, j, i % tiles_per_seq, 0))
        out_shape = jax.ShapeDtypeStruct((bsz, n // gw, seq, gw), jnp.bfloat16)
    return pl.pallas_call(
        functools.partial(_proj_kernel, scaled_tiles=scaled_cols // tn, scale=scale),
        name="proj",
        grid=(t // tm, n // tn),
        in_specs=[pl.BlockSpec((tm, d), lambda i, j: (i, 0)),
                  pl.BlockSpec((None, d, tn), lambda i, j: (layer, 0, j))],
        out_specs=out_spec,
        out_shape=out_shape,
        scratch_shapes=[pltpu.VMEM((tm, d), jnp.bfloat16)],
        compiler_params=pltpu.CompilerParams(
            dimension_semantics=("parallel", "arbitrary"), vmem_limit_bytes=VMEM_LIMIT),
    )(h, w)


def _proj_ln_kernel(a_ref, w_ref, h_ref, g_ref, b_ref, o_ref):
    half = a_ref.shape[0] // 2
    for rows in (slice(0, half), slice(half, 2 * half)):
        y = ALPHA * h_ref[rows, :] + jnp.dot(a_ref[rows, :], w_ref[...], preferred_element_type=jnp.float32)
        o_ref[rows, :] = _layer_norm(y, g_ref[...], b_ref[...])


def _proj_ln(a, w, layer, h, g, b):
    t, k = a.shape
    d = w.shape[2]
    tm = LN_ROWS
    return pl.pallas_call(
        _proj_ln_kernel,
        name="proj_ln",
        grid=(t // tm,),
        in_specs=[pl.BlockSpec((tm, k), lambda i: (i, 0)),
                  pl.BlockSpec((None, k, d), lambda i: (layer, 0, 0)),
                  pl.BlockSpec((tm, d), lambda i: (i, 0)),
                  pl.BlockSpec((1, d), lambda i: (0, 0)),
                  pl.BlockSpec((1, d), lambda i: (0, 0))],
        out_specs=pl.BlockSpec((tm, d), lambda i: (i, 0)),
        out_shape=jax.ShapeDtypeStruct((t, d), jnp.float32),
        compiler_params=pltpu.CompilerParams(
            dimension_semantics=("parallel",), vmem_limit_bytes=VMEM_LIMIT),
    )(a, w, h, g, b)


def _ffn_kernel(h_ref, halo_ref, wg_ref, wv_ref, cwg_ref, cwv_ref, cbg_ref, cbv_ref, wd_ref,
                g_ref, b_ref, o_ref, x_ref, acc_ref, ua_ref, ub_ref, *, seq_len):
    i = pl.program_id(0)
    f = pl.program_id(1)
    tm = h_ref.shape[0]

    @pl.when(f == 0)
    def _():
        at_start = (i * tm) % seq_len == 0
        halo = jnp.where(at_start, 0.0, halo_ref[...])
        x_ref[0:SUBLANES, :] = halo.astype(jnp.bfloat16)
        x_ref[SUBLANES:, :] = h_ref[...].astype(jnp.bfloat16)
        acc_ref[...] = jnp.zeros_like(acc_ref)

    x = x_ref[...]
    tf = wg_ref.shape[1]
    halves = [slice(0, tf // 2), slice(tf // 2, tf)]

    base = jnp.minimum(f, 0)
    u_refs = (ua_ref, ub_ref)
    for u_ref, cs in zip(u_refs, halves):
        u_ref[base] = jnp.dot(x, wg_ref[:, cs], preferred_element_type=jnp.float32)
        u_ref[base + 1] = jnp.dot(x, wv_ref[:, cs], preferred_element_type=jnp.float32)

    def conv(u, cw_ref, cb_ref, cs):
        u1 = pltpu.roll(u, 1, 0)
        u2 = pltpu.roll(u, 2, 0)
        return (u2[SUBLANES:] * cw_ref[0:1, cs] + u1[SUBLANES:] * cw_ref[1:2, cs]
                + u[SUBLANES:] * cw_ref[2:3, cs] + cb_ref[:, cs])

    acts = []
    for u_ref, cs in zip(u_refs, halves):
        gate = conv(u_ref[0], cwg_ref, cbg_ref, cs)
        val = conv(u_ref[1], cwv_ref, cbv_ref, cs)
        acts.append((jax.nn.gelu(gate) * val).astype(jnp.bfloat16))
    for act, cs in zip(acts, halves):
        acc_ref[...] += jnp.dot(act, wd_ref[cs, :], preferred_element_type=jnp.float32)

    @pl.when(f == pl.num_programs(1) - 1)
    def _():
        y = ALPHA * h_ref[...] + acc_ref[...]
        o_ref[...] = _layer_norm(y, g_ref[...], b_ref[...])


def _conv_ffn_ln(h, ffn_weights, layer, g, b, seq_len):
    t, d = h.shape
    tm, tf = FFN_ROWS, FF_TILE
    nf = D_FF_PAD // tf
    halo_blocks = tm // SUBLANES
    return pl.pallas_call(
        functools.partial(_ffn_kernel, seq_len=seq_len),
        name="conv_ffn_ln",
        grid=(t // tm, nf),
        in_specs=[pl.BlockSpec((tm, d), lambda i, f: (i, 0)),
                  pl.BlockSpec((SUBLANES, d), lambda i, f: (jnp.maximum(i * halo_blocks - 1, 0), 0)),
                  pl.BlockSpec((None, d, tf), lambda i, f: (layer, 0, f)),
                  pl.BlockSpec((None, d, tf), lambda i, f: (layer, 0, f)),
                  pl.BlockSpec((None, CONV_WIDTH, tf), lambda i, f: (layer, 0, f)),
                  pl.BlockSpec((None, CONV_WIDTH, tf), lambda i, f: (layer, 0, f)),
                  pl.BlockSpec((None, 1, tf), lambda i, f: (layer, 0, f)),
                  pl.BlockSpec((None, 1, tf), lambda i, f: (layer, 0, f)),
                  pl.BlockSpec((None, tf, d), lambda i, f: (layer, f, 0)),
                  pl.BlockSpec((1, d), lambda i, f: (0, 0)),
                  pl.BlockSpec((1, d), lambda i, f: (0, 0))],
        out_specs=pl.BlockSpec((tm, d), lambda i, f: (i, 0)),
        out_shape=jax.ShapeDtypeStruct((t, d), jnp.float32),
        scratch_shapes=[pltpu.VMEM((tm + SUBLANES, d), jnp.bfloat16),
                        pltpu.VMEM((tm, d), jnp.float32),
                        pltpu.VMEM((2, tm + SUBLANES, tf // 2), jnp.float32),
                        pltpu.VMEM((2, tm + SUBLANES, tf // 2), jnp.float32)],
        compiler_params=pltpu.CompilerParams(
            dimension_semantics=("parallel", "arbitrary"), vmem_limit_bytes=VMEM_LIMIT),
    )(h, h, *ffn_weights, g, b)


def _moba_kernel(q_ref, k_ref, v_ref, bias_ref, o_ref, *scratch):
    _moba_prepare(k_ref, v_ref, *scratch[:3])

    def body(c, carry):
        _moba_query_block(c, q_ref, k_ref, bias_ref, o_ref, *scratch)
        return carry

    lax.fori_loop(0, q_ref.shape[0] // MOBA_BLOCK, body, 0)


def _moba_prepare(k_ref, v_ref, kmean_ref, kparts_ref, vt_ref):
    n_heads, n_blocks = kmean_ref.shape[0], kmean_ref.shape[1]
    blk = MOBA_BLOCK
    cols = lambda j: slice(j * HEAD_DIM, (j + 1) * HEAD_DIM)
    rows = lax.broadcasted_iota(jnp.int32, (HEAD_DIM, HEAD_DIM), 0)
    lanes = lax.broadcasted_iota(jnp.int32, (HEAD_DIM, HEAD_DIM), 1)
    eye = (rows == lanes).astype(jnp.bfloat16)

    def build(n, carry):
        keys = pl.ds(pl.multiple_of(n * blk, blk), blk)
        for j in range(n_heads):
            kb = k_ref[keys, cols(j)]
            vb = v_ref[keys, cols(j)]
            kmean_ref[j, pl.ds(n, 1), :] = (
                jnp.sum(kb.astype(jnp.float32), axis=0, keepdims=True) * (1.0 / blk))
            vt_ref[j, n] = lax.dot_general(eye, vb, _NT,
                                           preferred_element_type=jnp.float32).astype(jnp.bfloat16)
        return carry

    lax.fori_loop(0, n_blocks, build, 0)
    for j in range(n_heads):
        km = kmean_ref[j]
        hi = km.astype(jnp.bfloat16)
        r1 = km - hi.astype(jnp.float32)
        mid = r1.astype(jnp.bfloat16)
        lo = (r1 - mid.astype(jnp.float32)).astype(jnp.bfloat16)
        kparts_ref[j] = jnp.concatenate([hi, mid, lo], axis=0)


def _moba_query_block(c, q_ref, k_ref, bias_ref, o_ref,
                      kmean_ref, kparts_ref, vt_ref, sel_ref, m_ref, l_ref, acc_ref, sa_ref, sb_ref):
    n_heads, n_blocks = kmean_ref.shape[0], kmean_ref.shape[1]
    blk = MOBA_BLOCK
    cols = lambda j: slice(j * HEAD_DIM, (j + 1) * HEAD_DIM)
    qrows = pl.ds(pl.multiple_of(c * blk, blk), blk)

    qs = [q_ref[qrows, cols(j)] for j in range(n_heads)]

    for j in range(n_heads):
        g3 = lax.dot_general(kparts_ref[j], qs[j], _NT, preferred_element_type=jnp.float32)
        gate = (g3[2 * n_blocks:] + g3[n_blocks:2 * n_blocks]) + g3[:n_blocks]
        row = lax.broadcasted_iota(jnp.int32, gate.shape, 0)
        past = row < c
        gate = jnp.where(past, gate, NEG)
        rank = jnp.zeros(gate.shape, jnp.int32)
        for m in range(n_blocks):
            gm = gate[m:m + 1, :]
            beats = (gm > gate) | ((gm == gate) & (row > m))
            rank = rank + beats.astype(jnp.int32)
        sel_ref[j] = (((rank < MOBA_TOPK) & past) | (row == c)).astype(jnp.float32)
        m_ref[j] = jnp.full(m_ref.shape[1:], NEG, jnp.float32)
        l_ref[j] = jnp.zeros(l_ref.shape[1:], jnp.float32)
        acc_ref[j] = jnp.zeros(acc_ref.shape[1:], jnp.float32)

    def score_matmuls(dst_ref, n):
        keys = pl.ds(pl.multiple_of(jnp.minimum(n, c) * blk, blk), blk)
        for j in range(n_heads):
            dst_ref[j] = lax.dot_general(k_ref[keys, cols(j)], qs[j], _NT,
                                         preferred_element_type=jnp.float32)

    def softmax_update(src_ref, n, far):
        live = n <= c
        nb_ = jnp.minimum(n, c)
        kind = jnp.clip(n - (c - 2), 0, 2)
        ps, alphas = [], []
        for j in range(n_heads):
            if far:
                s = src_ref[j]
                shift = bias_ref[j, 0, 0:1, :]
            else:
                s = src_ref[j] + bias_ref[j, kind]
                shift = 0.0
            chosen = jnp.logical_and(sel_ref[j, pl.ds(nb_, 1), :] > 0.5, live)
            m_old = m_ref[j]
            m_new = jnp.where(chosen, jnp.maximum(m_old, jnp.max(s, axis=0, keepdims=True) + shift), m_old)
            p = jnp.exp2(s - jnp.where(chosen, m_new - shift, BIG))
            alphas.append(jnp.exp2(m_old - m_new))
            m_ref[j] = m_new
            l_ref[j] = alphas[j] * l_ref[j] + jnp.sum(p, axis=0, keepdims=True)
            ps.append(p.astype(jnp.bfloat16))
        for j in range(n_heads):
            acc_ref[j] = alphas[j] * acc_ref[j] + jnp.dot(
                vt_ref[j, nb_], ps[j], preferred_element_type=jnp.float32)

    score_matmuls(sa_ref, 0)

    def far_pair(i, carry):
        n = 2 * i
        score_matmuls(sb_ref, n + 1)
        softmax_update(sa_ref, n, True)
        score_matmuls(sa_ref, n + 2)
        softmax_update(sb_ref, n + 1, True)
        return carry

    def last_pair(first_ref, second_ref, n):
        score_matmuls(second_ref, n + 1)
        softmax_update(first_ref, n, False)
        softmax_update(second_ref, n + 1, False)

    n_far = jnp.maximum(c - 1, 0)
    far_pairs = n_far // 2
    lax.fori_loop(0, far_pairs, far_pair, 0)
    n0 = 2 * far_pairs
    odd_far = n_far % 2 == 1

    @pl.when(jnp.logical_not(odd_far))
    def _():
        last_pair(sa_ref, sb_ref, n0)

    @pl.when(odd_far)
    def _():
        score_matmuls(sb_ref, n0 + 1)
        softmax_update(sa_ref, n0, True)
        last_pair(sb_ref, sa_ref, n0 + 1)

    for j in range(n_heads):
        out_t = acc_ref[j] / l_ref[j]
        o_ref[0, qrows, cols(j)] = out_t.T.astype(o_ref.dtype)


def _moba_attention(qkv, bias_tiles):
    hpb = MOBA_HEADS_PER_STEP
    ng = N_HEADS // hpb
    gw = hpb * HEAD_DIM
    bsz, _, s, _ = qkv.shape
    assert qkv.shape == (bsz, 3 * ng, s, gw)
    nb = s // MOBA_BLOCK
    return pl.pallas_call(
        _moba_kernel,
        name="moba_attention",
        grid=(bsz, ng),
        in_specs=[pl.BlockSpec((None, None, s, gw), lambda b, g: (b, g, 0, 0)),
                  pl.BlockSpec((None, None, s, gw), lambda b, g: (b, ng + g, 0, 0)),
                  pl.BlockSpec((None, None, s, gw), lambda b, g: (b, 2 * ng + g, 0, 0)),
                  pl.BlockSpec((hpb, 3, MOBA_BLOCK, MOBA_BLOCK), lambda b, g: (g, 0, 0, 0))],
        out_specs=pl.BlockSpec((1, s, gw), lambda b, g: (b, 0, g)),
        out_shape=jax.ShapeDtypeStruct((bsz, s, ATT_WIDTH), jnp.bfloat16),
        scratch_shapes=[pltpu.VMEM((hpb, nb, HEAD_DIM), jnp.float32),
                        pltpu.VMEM((hpb, 3 * nb, HEAD_DIM), jnp.bfloat16),
                        pltpu.VMEM((hpb, nb, HEAD_DIM, MOBA_BLOCK), jnp.bfloat16),
                        pltpu.VMEM((hpb, nb, MOBA_BLOCK), jnp.float32),
                        pltpu.VMEM((hpb, 1, MOBA_BLOCK), jnp.float32),
                        pltpu.VMEM((hpb, 1, MOBA_BLOCK), jnp.float32),
                        pltpu.VMEM((hpb, HEAD_DIM, MOBA_BLOCK), jnp.float32),
                        pltpu.VMEM((hpb, MOBA_BLOCK, MOBA_BLOCK), jnp.float32),
                        pltpu.VMEM((hpb, MOBA_BLOCK, MOBA_BLOCK), jnp.float32)],
        compiler_params=pltpu.CompilerParams(
            dimension_semantics=("parallel", "parallel"), vmem_limit_bytes=VMEM_LIMIT),
    )(qkv, qkv, qkv, bias_tiles)


def _bias_by_bucket(rel_bias, bucket):
    onehot = (bucket[None] == jnp.arange(NUM_BUCKETS).reshape((NUM_BUCKETS,) + (1,) * bucket.ndim))
    return jnp.einsum('kh,k...->h...', rel_bias.astype(jnp.float32), onehot.astype(jnp.float32),
                      precision=lax.Precision.HIGHEST)


def _moba_bias_tables(rel_bias):
    key = jnp.arange(MOBA_BLOCK)[:, None]
    qry = jnp.arange(MOBA_BLOCK)[None, :]
    d_own = qry - key
    own = jnp.where(d_own >= 0, _bias_by_bucket(rel_bias, _t5_bucket(d_own)), NEG)
    prev = _bias_by_bucket(rel_bias, _t5_bucket(d_own + MOBA_BLOCK))
    far = jnp.broadcast_to(rel_bias.astype(jnp.float32)[NUM_BUCKETS - 1][:, None, None], own.shape)
    return jnp.stack([far, prev, own], axis=1) * LOG2E


def _swa_kernel(q_ref, kvp_ref, kvc_ref, bias_ref, sink_ref, o_ref):
    c = pl.program_id(1)
    has_prev = c > 0
    blk = SWA_BLOCK
    rows = lax.broadcasted_iota(jnp.int32, (HEAD_DIM, HEAD_DIM), 0)
    lanes = lax.broadcasted_iota(jnp.int32, (HEAD_DIM, HEAD_DIM), 1)
    eye = (rows == lanes).astype(jnp.bfloat16)
    head = lambda hh: slice(hh * HEAD_DIM, (hh + 1) * HEAD_DIM)
    kv_cols = lambda g, is_v: slice((is_v * N_KV_HEADS + g) * HEAD_DIM, (is_v * N_KV_HEADS + g + 1) * HEAD_DIM)

    scores, values = [], []
    for g in range(N_KV_HEADS):
        qg = jnp.concatenate([q_ref[0, :, head(g * Q_GROUP + i)] for i in range(Q_GROUP)], axis=0)
        scores.append(tuple(
            lax.dot_general(ref[0, :, kv_cols(g, 0)], qg, _NT, preferred_element_type=jnp.float32)
            for ref in (kvp_ref, kvc_ref)))
        values.append(tuple(
            lax.dot_general(eye, ref[0, :, kv_cols(g, 1)], _NT,
                            preferred_element_type=jnp.float32).astype(jnp.bfloat16)
            for ref in (kvp_ref, kvc_ref)))

    probs = []
    for g in range(N_KV_HEADS):
        s_prev = jnp.where(has_prev, scores[g][0] + bias_ref[g, :blk, :], NEG)
        s_cur = scores[g][1] + bias_ref[g, blk:, :]
        sink = sink_ref[g]
        m = jnp.maximum(jnp.maximum(jnp.max(s_prev, axis=0, keepdims=True),
                                    jnp.max(s_cur, axis=0, keepdims=True)), sink)
        p_prev = jnp.exp2(s_prev - m)
        p_cur = jnp.exp2(s_cur - m)
        denom = (jnp.sum(p_prev, axis=0, keepdims=True) + jnp.sum(p_cur, axis=0, keepdims=True)
                 + jnp.exp2(sink - m))
        probs.append((p_prev.astype(jnp.bfloat16), p_cur.astype(jnp.bfloat16), denom))

    for g in range(N_KV_HEADS):
        p_prev, p_cur, denom = probs[g]
        out_t = (jnp.dot(values[g][0], p_prev, preferred_element_type=jnp.float32)
                 + jnp.dot(values[g][1], p_cur, preferred_element_type=jnp.float32)) / denom
        for i in range(Q_GROUP):
            o_ref[0, :, head(g * Q_GROUP + i)] = out_t[:, i * blk:(i + 1) * blk].T.astype(o_ref.dtype)


def _swa_attention(q, kv, band_bias, sinks):
    bsz, s, _ = q.shape
    nq = s // SWA_BLOCK
    gq = Q_GROUP * SWA_BLOCK
    prev = lambda c: jnp.maximum(c - 1, 0)
    return pl.pallas_call(
        _swa_kernel,
        name="swa_attention",
        grid=(bsz, nq),
        in_specs=[pl.BlockSpec((1, SWA_BLOCK, ATT_WIDTH), lambda b, c: (b, c, 0)),
                  pl.BlockSpec((1, SWA_BLOCK, 2 * KV_WIDTH), lambda b, c: (b, prev(c), 0)),
                  pl.BlockSpec((1, SWA_BLOCK, 2 * KV_WIDTH), lambda b, c: (b, c, 0)),
                  pl.BlockSpec((N_KV_HEADS, 2 * SWA_BLOCK, gq), lambda b, c: (0, 0, 0)),
                  pl.BlockSpec((N_KV_HEADS, 1, gq), lambda b, c: (0, 0, 0))],
        out_specs=pl.BlockSpec((1, SWA_BLOCK, ATT_WIDTH), lambda b, c: (b, c, 0)),
        out_shape=jax.ShapeDtypeStruct((bsz, s, ATT_WIDTH), jnp.bfloat16),
        compiler_params=pltpu.CompilerParams(
            dimension_semantics=("parallel", "arbitrary"), vmem_limit_bytes=VMEM_LIMIT),
    )(q, kv, kv, band_bias, sinks)


def _swa_band_bias(rel_bias):
    qi = jnp.arange(SWA_BLOCK)[:, None]
    kj = jnp.arange(2 * SWA_BLOCK)[None, :]
    dist = qi + SWA_BLOCK - kj
    in_band = (dist >= 0) & (dist < SWA_WINDOW)
    bias = jnp.where(in_band, _bias_by_bucket(rel_bias, _t5_bucket(dist)) * LOG2E, NEG)
    bias = bias.reshape(N_KV_HEADS, Q_GROUP, SWA_BLOCK, 2 * SWA_BLOCK)
    return bias.transpose(0, 3, 1, 2).reshape(N_KV_HEADS, 2 * SWA_BLOCK, Q_GROUP * SWA_BLOCK)


def _swa_sink_rows(sinks):
    rows = jnp.repeat(sinks.astype(jnp.float32) * LOG2E, SWA_BLOCK)
    return rows.reshape(N_KV_HEADS, 1, Q_GROUP * SWA_BLOCK)


def _pad_ffn_weights(w_up, conv_w, conv_b, w_down):
    pad = D_FF_PAD - D_FF

    def halves(a, dtype):
        return tuple(jnp.pad(part.astype(dtype), ((0, 0), (0, 0), (0, pad)))
                     for part in (a[..., :D_FF], a[..., D_FF:]))

    return (halves(w_up, jnp.bfloat16) + halves(conv_w, jnp.float32) + halves(conv_b[:, None, :], jnp.float32)
            + (jnp.pad(w_down.astype(jnp.bfloat16), ((0, 0), (0, pad), (0, 0))),))


def kernel(x, rel_bias, moba_w_qkv, moba_w_o, swa_w_kv, swa_w_q, swa_sinks, swa_w_o,
           ffn_w_up, ffn_conv_w, ffn_conv_b, ffn_w_down, ln_g, ln_b):
    bsz, s, d = x.shape
    bf16 = jnp.bfloat16
    moba_bias = _moba_bias_tables(rel_bias)
    band_bias = _swa_band_bias(rel_bias)
    w_qkv, w_o_moba = moba_w_qkv.astype(bf16), moba_w_o.astype(bf16)
    w_kv, w_q, w_o_swa = swa_w_kv.astype(bf16)[None], swa_w_q.astype(bf16), swa_w_o.astype(bf16)
    ffn_weights = _pad_ffn_weights(ffn_w_up, ffn_conv_w, ffn_conv_b, ffn_w_down)

    h = x.reshape(bsz * s, d)
    kv = None
    for layer in range(DEPTH):
        if layer < N_A_LAYERS:
            qkv = _proj(h, w_qkv, layer, scaled_cols=ATT_WIDTH, scale=SCALE * LOG2E,
                        group=(bsz, s, MOBA_HEADS_PER_STEP * HEAD_DIM))
            attn = _moba_attention(qkv, moba_bias)
            w_o, j = w_o_moba, layer
        else:
            j = layer - N_A_LAYERS
            if kv is None:
                kv = _proj(h, w_kv, 0).reshape(bsz, s, 2 * KV_WIDTH)
            q = _proj(h, w_q, j, scaled_cols=ATT_WIDTH, scale=SCALE * LOG2E).reshape(bsz, s, ATT_WIDTH)
            attn = _swa_attention(q, kv, band_bias, _swa_sink_rows(swa_sinks[j]))
            w_o = w_o_swa
        h = _proj_ln(attn.reshape(bsz * s, ATT_WIDTH), w_o, j, h,
                     ln_g[layer, 0][None, :], ln_b[layer, 0][None, :])
        h = _conv_ffn_ln(h, ffn_weights, layer,
                         ln_g[layer, 1][None, :], ln_b[layer, 1][None, :], s)
    return h.reshape(bsz, s, d)
```

```python
import functools
import math

import jax
import jax.numpy as jnp
from jax import lax
from jax.experimental import pallas as pl
from jax.experimental.pallas import tpu as pltpu

D_MODEL = 2048
N_HEADS = 16
HEAD_DIM = 128
N_KV_HEADS = 4
Q_GROUP = N_HEADS // N_KV_HEADS
ATT_WIDTH = N_HEADS * HEAD_DIM
KV_WIDTH = N_KV_HEADS * HEAD_DIM
MOBA_BLOCK = 256
MOBA_TOPK = 3
SWA_WINDOW = 128
SWA_BLOCK = 128
NUM_BUCKETS = 32
MAX_DISTANCE = 128
D_FF = 5504
CONV_WIDTH = 3
DEPTH = 4
N_A_LAYERS = DEPTH // 2
ALPHA = (2.0 * DEPTH) ** 0.25
LN_EPS = 1e-5
NEG = -1e30
BIG = 1e30
SCALE = HEAD_DIM ** -0.5
LOG2E = math.log2(math.e)

V7X_VMEM_BYTES = 64 * 1024 * 1024
VMEM_LIMIT = 56 * 1024 * 1024
SUBLANES = 8
FFN_PIECE = 256
FF_TILE = 4 * FFN_PIECE
D_FF_PAD = -(-D_FF // FF_TILE) * FF_TILE
FFN_ROWS = 512
PROJ_ROWS = 1024
PROJ_COLS = 1024
LN_ROWS = 512
MOBA_HEADS_PER_STEP = 4

_NT = (((1,), (1,)), ((), ()))


def _t5_bucket(dist):
    n = jnp.maximum(dist, 0)
    max_exact = NUM_BUCKETS // 2
    nf = jnp.maximum(n, 1).astype(jnp.float32)
    large = max_exact + (jnp.log(nf / max_exact) / math.log(MAX_DISTANCE / max_exact)
                         * (NUM_BUCKETS - max_exact)).astype(jnp.int32)
    large = jnp.minimum(large, NUM_BUCKETS - 1)
    return jnp.where(n < max_exact, n, large)


def _layer_norm(y, g, b):
    mu = jnp.mean(y, axis=-1, keepdims=True)
    yc = y - mu
    var = jnp.mean(yc * yc, axis=-1, keepdims=True)
    return yc * lax.rsqrt(var + LN_EPS) * g + b


def _proj_kernel(h_ref, w_ref, o_ref, xb_ref, *, scaled_tiles, scale):
    j = pl.program_id(1)

    @pl.when(j == 0)
    def _():
        xb_ref[...] = h_ref[...].astype(jnp.bfloat16)

    r = jnp.dot(xb_ref[...], w_ref[...], preferred_element_type=jnp.float32)
    if scaled_tiles:
        r = r * jnp.where(j < scaled_tiles, scale, 1.0)
    if len(o_ref.shape) == 3:
        gw = o_ref.shape[2]
        for k in range(o_ref.shape[0]):
            o_ref[k] = r[:, k * gw:(k + 1) * gw].astype(o_ref.dtype)
    else:
        o_ref[...] = r.astype(o_ref.dtype)


def _proj(h, w, layer, scaled_cols=0, scale=1.0, group=None):
    t, d = h.shape
    n = w.shape[2]
    tm, tn = PROJ_ROWS, min(PROJ_COLS, n)
    assert scaled_cols % tn == 0
    if group is None:
        out_spec = pl.BlockSpec((tm, tn), lambda i, j: (i, j))
        out_shape = jax.ShapeDtypeStruct((t, n), jnp.bfloat16)
    else:
        bsz, seq, gw = group
        tiles_per_seq = seq // tm
        assert bsz * seq == t and seq % tm == 0 and tn % gw == 0
        out_spec = pl.BlockSpec((None, tn // gw, tm, gw),
                                lambda i, j: (i // tiles_per_seq, j, i % tiles_per_seq, 0))
        out_shape = jax.ShapeDtypeStruct((bsz, n // gw, seq, gw), jnp.bfloat16)
    return pl.pallas_call(
        functools.partial(_proj_kernel, scaled_tiles=scaled_cols // tn, scale=scale),
        name="proj",
        grid=(t // tm, n // tn),
        in_specs=[pl.BlockSpec((tm, d), lambda i, j: (i, 0)),
                  pl.BlockSpec((None, d, tn), lambda i, j: (layer, 0, j))],
        out_specs=out_spec,
        out_shape=out_shape,
        scratch_shapes=[pltpu.VMEM((tm, d), jnp.bfloat16)],
        compiler_params=pltpu.CompilerParams(
            dimension_semantics=("parallel", "arbitrary"), vmem_limit_bytes=VMEM_LIMIT),
    )(h, w)


def _proj_ln_kernel(a_ref, w_ref, h_ref, g_ref, b_ref, o_ref):
    half = a_ref.shape[0] // 2
    for rows in (slice(0, half), slice(half, 2 * half)):
        y = ALPHA * h_ref[rows, :] + jnp.dot(a_ref[rows, :], w_ref[...], preferred_element_type=jnp.float32)
        o_ref[rows, :] = _layer_norm(y, g_ref[...], b_ref[...])


def _proj_ln(a, w, layer, h, g, b):
    t, k = a.shape
    d = w.shape[2]
    tm = LN_ROWS
    return pl.pallas_call(
        _proj_ln_kernel,
        name="proj_ln",
        grid=(t // tm,),
        in_specs=[pl.BlockSpec((tm, k), lambda i: (i, 0)),
                  pl.BlockSpec((None, k, d), lambda i: (layer, 0, 0)),
                  pl.BlockSpec((tm, d), lambda i: (i, 0)),
                  pl.BlockSpec((1, d), lambda i: (0, 0)),
                  pl.BlockSpec((1, d), lambda i: (0, 0))],
        out_specs=pl.BlockSpec((tm, d), lambda i: (i, 0)),
        out_shape=jax.ShapeDtypeStruct((t, d), jnp.float32),
        compiler_params=pltpu.CompilerParams(
            dimension_semantics=("parallel",), vmem_limit_bytes=VMEM_LIMIT),
    )(a, w, h, g, b)


def _ffn_kernel(h_ref, halo_ref, wg_ref, wv_ref, cwg_ref, cwv_ref, cbg_ref, cbv_ref, wd_ref,
                g_ref, b_ref, o_ref, x_ref, acc_ref, *u_refs, seq_len, last_pieces):
    i = pl.program_id(0)
    f = pl.program_id(1)
    tm = h_ref.shape[0]
    last = pl.num_programs(1) - 1

    @pl.when(f == 0)
    def _():
        at_start = (i * tm) % seq_len == 0
        halo = jnp.where(at_start, 0.0, halo_ref[...])
        x_ref[0:SUBLANES, :] = halo.astype(jnp.bfloat16)
        x_ref[SUBLANES:, :] = h_ref[...].astype(jnp.bfloat16)
        acc_ref[...] = jnp.zeros_like(acc_ref)

    def conv(u, cw_ref, cb_ref, cs):
        u1 = pltpu.roll(u, 1, 0)
        u2 = pltpu.roll(u, 2, 0)
        return (u2[SUBLANES:] * cw_ref[0:1, cs] + u1[SUBLANES:] * cw_ref[1:2, cs]
                + u[SUBLANES:] * cw_ref[2:3, cs] + cb_ref[:, cs])

    def pieces(n):
        x = x_ref[...]
        cols = [slice(k * FFN_PIECE, (k + 1) * FFN_PIECE) for k in range(n)]
        base = jnp.minimum(f, 0)
        for u_ref, cs in zip(u_refs, cols):
            u_ref[base] = jnp.dot(x, wg_ref[:, cs], preferred_element_type=jnp.float32)
            u_ref[base + 1] = jnp.dot(x, wv_ref[:, cs], preferred_element_type=jnp.float32)
        acts = []
        for u_ref, cs in zip(u_refs, cols):
            gate = conv(u_ref[0], cwg_ref, cbg_ref, cs)
            val = conv(u_ref[1], cwv_ref, cbv_ref, cs)
            acts.append((jax.nn.gelu(gate) * val).astype(jnp.bfloat16))
        for act, cs in zip(acts, cols):
            acc_ref[...] += jnp.dot(act, wd_ref[cs, :], preferred_element_type=jnp.float32)

    @pl.when(f < last)
    def _():
        pieces(len(u_refs))

    @pl.when(f == last)
    def _():
        pieces(last_pieces)
        y = ALPHA * h_ref[...] + acc_ref[...]
        o_ref[...] = _layer_norm(y, g_ref[...], b_ref[...])


def _conv_ffn_ln(h, ffn_weights, layer, g, b, seq_len):
    t, d = h.shape
    tm, tf = FFN_ROWS, FF_TILE
    nf = D_FF_PAD // tf
    pieces_per_step = tf // FFN_PIECE
    last_pieces = -(-D_FF // FFN_PIECE) - (nf - 1) * pieces_per_step
    halo_blocks = tm // SUBLANES
    return pl.pallas_call(
        functools.partial(_ffn_kernel, seq_len=seq_len, last_pieces=last_pieces),
        name="conv_ffn_ln",
        grid=(t // tm, nf),
        in_specs=[pl.BlockSpec((tm, d), lambda i, f: (i, 0)),
                  pl.BlockSpec((SUBLANES, d), lambda i, f: (jnp.maximum(i * halo_blocks - 1, 0), 0)),
                  pl.BlockSpec((None, d, tf), lambda i, f: (layer, 0, f)),
                  pl.BlockSpec((None, d, tf), lambda i, f: (layer, 0, f)),
                  pl.BlockSpec((None, CONV_WIDTH, tf), lambda i, f: (layer, 0, f)),
                  pl.BlockSpec((None, CONV_WIDTH, tf), lambda i, f: (layer, 0, f)),
                  pl.BlockSpec((None, 1, tf), lambda i, f: (layer, 0, f)),
                  pl.BlockSpec((None, 1, tf), lambda i, f: (layer, 0, f)),
                  pl.BlockSpec((None, tf, d), lambda i, f: (layer, f, 0)),
                  pl.BlockSpec((1, d), lambda i, f: (0, 0)),
                  pl.BlockSpec((1, d), lambda i, f: (0, 0))],
        out_specs=pl.BlockSpec((tm, d), lambda i, f: (i, 0)),
        out_shape=jax.ShapeDtypeStruct((t, d), jnp.float32),
        scratch_shapes=[pltpu.VMEM((tm + SUBLANES, d), jnp.bfloat16),
                        pltpu.VMEM((tm, d), jnp.float32)]
        + [pltpu.VMEM((2, tm + SUBLANES, FFN_PIECE), jnp.float32)] * pieces_per_step,
        compiler_params=pltpu.CompilerParams(
            dimension_semantics=("parallel", "arbitrary"), vmem_limit_bytes=VMEM_LIMIT),
    )(h, h, *ffn_weights, g, b)


def _moba_kernel(q_ref, k_ref, v_ref, bias_ref, o_ref, *scratch):
    _moba_prepare(k_ref, v_ref, *scratch[:3])

    def body(c, carry):
        _moba_query_block(c, q_ref, k_ref, bias_ref, o_ref, *scratch)
        return carry

    lax.fori_loop(0, q_ref.shape[0] // MOBA_BLOCK, body, 0)


def _moba_prepare(k_ref, v_ref, kmean_ref, kparts_ref, vt_ref):
    n_heads, n_blocks = kmean_ref.shape[0], kmean_ref.shape[1]
    blk = MOBA_BLOCK
    cols = lambda j: slice(j * HEAD_DIM, (j + 1) * HEAD_DIM)
    rows = lax.broadcasted_iota(jnp.int32, (HEAD_DIM, HEAD_DIM), 0)
    lanes = lax.broadcasted_iota(jnp.int32, (HEAD_DIM, HEAD_DIM), 1)
    eye = (rows == lanes).astype(jnp.bfloat16)

    def build(n, carry):
        keys = pl.ds(pl.multiple_of(n * blk, blk), blk)
        for j in range(n_heads):
            kb = k_ref[keys, cols(j)]
            vb = v_ref[keys, cols(j)]
            kmean_ref[j, pl.ds(n, 1), :] = (
                jnp.sum(kb.astype(jnp.float32), axis=0, keepdims=True) * (1.0 / blk))
            vt_ref[j, n] = lax.dot_general(eye, vb, _NT,
                                           preferred_element_type=jnp.float32).astype(jnp.bfloat16)
        return carry

    lax.fori_loop(0, n_blocks, build, 0)
    for j in range(n_heads):
        km = kmean_ref[j]
        hi = km.astype(jnp.bfloat16)
        r1 = km - hi.astype(jnp.float32)
        mid = r1.astype(jnp.bfloat16)
        lo = (r1 - mid.astype(jnp.float32)).astype(jnp.bfloat16)
        kparts_ref[j] = jnp.concatenate([hi, mid, lo], axis=0)


def _moba_query_block(c, q_ref, k_ref, bias_ref, o_ref,
                      kmean_ref, kparts_ref, vt_ref, sel_ref, m_ref, l_ref, acc_ref, sa_ref, sb_ref):
    n_heads, n_blocks = kmean_ref.shape[0], kmean_ref.shape[1]
    blk = MOBA_BLOCK
    cols = lambda j: slice(j * HEAD_DIM, (j + 1) * HEAD_DIM)
    qrows = pl.ds(pl.multiple_of(c * blk, blk), blk)

    qs = [q_ref[qrows, cols(j)] for j in range(n_heads)]

    for j in range(n_heads):
        g3 = lax.dot_general(kparts_ref[j], qs[j], _NT, preferred_element_type=jnp.float32)
        gate = (g3[2 * n_blocks:] + g3[n_blocks:2 * n_blocks]) + g3[:n_blocks]
        row = lax.broadcasted_iota(jnp.int32, gate.shape, 0)
        past = row < c
        gate = jnp.where(past, gate, NEG)
        rank = jnp.zeros(gate.shape, jnp.int32)
        for m in range(n_blocks):
            gm = gate[m:m + 1, :]
            beats = (gm > gate) | ((gm == gate) & (row > m))
            rank = rank + beats.astype(jnp.int32)
        sel_ref[j] = (((rank < MOBA_TOPK) & past) | (row == c)).astype(jnp.float32)
        m_ref[j] = jnp.full(m_ref.shape[1:], NEG, jnp.float32)
        l_ref[j] = jnp.zeros(l_ref.shape[1:], jnp.float32)
        acc_ref[j] = jnp.zeros(acc_ref.shape[1:], jnp.float32)

    def score_matmuls(dst_ref, n):
        keys = pl.ds(pl.multiple_of(jnp.minimum(n, c) * blk, blk), blk)
        for j in range(n_heads):
            dst_ref[j] = lax.dot_general(k_ref[keys, cols(j)], qs[j], _NT,
                                         preferred_element_type=jnp.float32)

    def softmax_update(src_ref, n, far):
        live = n <= c
        nb_ = jnp.minimum(n, c)
        kind = jnp.clip(n - (c - 2), 0, 2)
        ps, alphas = [], []
        for j in range(n_heads):
            if far:
                s = src_ref[j]
                shift = bias_ref[j, 0, 0:1, :]
            else:
                s = src_ref[j] + bias_ref[j, kind]
                shift = 0.0
            chosen = jnp.logical_and(sel_ref[j, pl.ds(nb_, 1), :] > 0.5, live)
            m_old = m_ref[j]
            m_new = jnp.where(chosen, jnp.maximum(m_old, jnp.max(s, axis=0, keepdims=True) + shift), m_old)
            p = jnp.exp2(s - jnp.where(chosen, m_new - shift, BIG))
            alphas.append(jnp.exp2(m_old - m_new))
            m_ref[j] = m_new
            l_ref[j] = alphas[j] * l_ref[j] + jnp.sum(p, axis=0, keepdims=True)
            ps.append(p.astype(jnp.bfloat16))
        for j in range(n_heads):
            acc_ref[j] = alphas[j] * acc_ref[j] + jnp.dot(
                vt_ref[j, nb_], ps[j], preferred_element_type=jnp.float32)

    score_matmuls(sa_ref, 0)

    def far_pair(i, carry):
        n = 2 * i
        score_matmuls(sb_ref, n + 1)
        softmax_update(sa_ref, n, True)
        score_matmuls(sa_ref, n + 2)
        softmax_update(sb_ref, n + 1, True)
        return carry

    def last_pair(first_ref, second_ref, n):
        score_matmuls(second_ref, n + 1)
        softmax_update(first_ref, n, False)
        softmax_update(second_ref, n + 1, False)

    def far_quad(q, carry):
        far_pair(2 * q, carry)
        return far_pair(2 * q + 1, carry)

    n_far = jnp.maximum(c - 1, 0)
    far_pairs = n_far // 2
    far_quads = far_pairs // 2
    lax.fori_loop(0, far_quads, far_quad, 0)
    lax.fori_loop(2 * far_quads, far_pairs, far_pair, 0)
    n0 = 2 * far_pairs
    odd_far = n_far % 2 == 1

    @pl.when(jnp.logical_not(odd_far))
    def _():
        last_pair(sa_ref, sb_ref, n0)

    @pl.when(odd_far)
    def _():
        score_matmuls(sb_ref, n0 + 1)
        softmax_update(sa_ref, n0, True)
        last_pair(sb_ref, sa_ref, n0 + 1)

    for j in range(n_heads):
        out_t = acc_ref[j] / l_ref[j]
        o_ref[0, qrows, cols(j)] = out_t.T.astype(o_ref.dtype)


def _moba_attention(qkv, bias_tiles):
    hpb = MOBA_HEADS_PER_STEP
    ng = N_HEADS // hpb
    gw = hpb * HEAD_DIM
    bsz, _, s, _ = qkv.shape
    assert qkv.shape == (bsz, 3 * ng, s, gw)
    nb = s // MOBA_BLOCK
    return pl.pallas_call(
        _moba_kernel,
        name="moba_attention",
        grid=(bsz, ng),
        in_specs=[pl.BlockSpec((None, None, s, gw), lambda b, g: (b, g, 0, 0)),
                  pl.BlockSpec((None, None, s, gw), lambda b, g: (b, ng + g, 0, 0)),
                  pl.BlockSpec((None, None, s, gw), lambda b, g: (b, 2 * ng + g, 0, 0)),
                  pl.BlockSpec((hpb, 3, MOBA_BLOCK, MOBA_BLOCK), lambda b, g: (g, 0, 0, 0))],
        out_specs=pl.BlockSpec((1, s, gw), lambda b, g: (b, 0, g)),
        out_shape=jax.ShapeDtypeStruct((bsz, s, ATT_WIDTH), jnp.bfloat16),
        scratch_shapes=[pltpu.VMEM((hpb, nb, HEAD_DIM), jnp.float32),
                        pltpu.VMEM((hpb, 3 * nb, HEAD_DIM), jnp.bfloat16),
                        pltpu.VMEM((hpb, nb, HEAD_DIM, MOBA_BLOCK), jnp.bfloat16),
                        pltpu.VMEM((hpb, nb, MOBA_BLOCK), jnp.float32),
                        pltpu.VMEM((hpb, 1, MOBA_BLOCK), jnp.float32),
                        pltpu.VMEM((hpb, 1, MOBA_BLOCK), jnp.float32),
                        pltpu.VMEM((hpb, HEAD_DIM, MOBA_BLOCK), jnp.float32),
                        pltpu.VMEM((hpb, MOBA_BLOCK, MOBA_BLOCK), jnp.float32),
                        pltpu.VMEM((hpb, MOBA_BLOCK, MOBA_BLOCK), jnp.float32)],
        compiler_params=pltpu.CompilerParams(
            dimension_semantics=("parallel", "parallel"), vmem_limit_bytes=VMEM_LIMIT),
    )(qkv, qkv, qkv, bias_tiles)


def _bias_by_bucket(rel_bias, bucket):
    onehot = (bucket[None] == jnp.arange(NUM_BUCKETS).reshape((NUM_BUCKETS,) + (1,) * bucket.ndim))
    return jnp.einsum('kh,k...->h...', rel_bias.astype(jnp.float32), onehot.astype(jnp.float32),
                      precision=lax.Precision.HIGHEST)


def _moba_bias_tables(rel_bias):
    key = jnp.arange(MOBA_BLOCK)[:, None]
    qry = jnp.arange(MOBA_BLOCK)[None, :]
    d_own = qry - key
    own = jnp.where(d_own >= 0, _bias_by_bucket(rel_bias, _t5_bucket(d_own)), NEG)
    prev = _bias_by_bucket(rel_bias, _t5_bucket(d_own + MOBA_BLOCK))
    far = jnp.broadcast_to(rel_bias.astype(jnp.float32)[NUM_BUCKETS - 1][:, None, None], own.shape)
    return jnp.stack([far, prev, own], axis=1) * LOG2E


def _swa_kernel(q_ref, kvp_ref, kvc_ref, bias_ref, sink_ref, o_ref):
    c = pl.program_id(1)
    has_prev = c > 0
    blk = SWA_BLOCK
    rows = lax.broadcasted_iota(jnp.int32, (HEAD_DIM, HEAD_DIM), 0)
    lanes = lax.broadcasted_iota(jnp.int32, (HEAD_DIM, HEAD_DIM), 1)
    eye = (rows == lanes).astype(jnp.bfloat16)
    head = lambda hh: slice(hh * HEAD_DIM, (hh + 1) * HEAD_DIM)
    kv_cols = lambda g, is_v: slice((is_v * N_KV_HEADS + g) * HEAD_DIM, (is_v * N_KV_HEADS + g + 1) * HEAD_DIM)

    scores, values = [], []
    for g in range(N_KV_HEADS):
        qg = jnp.concatenate([q_ref[0, :, head(g * Q_GROUP + i)] for i in range(Q_GROUP)], axis=0)
        scores.append(tuple(
            lax.dot_general(ref[0, :, kv_cols(g, 0)], qg, _NT, preferred_element_type=jnp.float32)
            for ref in (kvp_ref, kvc_ref)))
        values.append(tuple(
            lax.dot_general(eye, ref[0, :, kv_cols(g, 1)], _NT,
                            preferred_element_type=jnp.float32).astype(jnp.bfloat16)
            for ref in (kvp_ref, kvc_ref)))

    probs = []
    for g in range(N_KV_HEADS):
        s_prev = jnp.where(has_prev, scores[g][0] + bias_ref[g, :blk, :], NEG)
        s_cur = scores[g][1] + bias_ref[g, blk:, :]
        sink = sink_ref[g]
        m = jnp.maximum(jnp.maximum(jnp.max(s_prev, axis=0, keepdims=True),
                                    jnp.max(s_cur, axis=0, keepdims=True)), sink)
        p_prev = jnp.exp2(s_prev - m)
        p_cur = jnp.exp2(s_cur - m)
        denom = (jnp.sum(p_prev, axis=0, keepdims=True) + jnp.sum(p_cur, axis=0, keepdims=True)
                 + jnp.exp2(sink - m))
        probs.append((p_prev.astype(jnp.bfloat16), p_cur.astype(jnp.bfloat16), denom))

    for g in range(N_KV_HEADS):
        p_prev, p_cur, denom = probs[g]
        out_t = (jnp.dot(values[g][0], p_prev, preferred_element_type=jnp.float32)
                 + jnp.dot(values[g][1], p_cur, preferred_element_type=jnp.float32)) / denom
        for i in range(Q_GROUP):
            o_ref[0, :, head(g * Q_GROUP + i)] = out_t[:, i * blk:(i + 1) * blk].T.astype(o_ref.dtype)


def _swa_attention(q, kv, band_bias, sinks):
    bsz, s, _ = q.shape
    nq = s // SWA_BLOCK
    gq = Q_GROUP * SWA_BLOCK
    prev = lambda c: jnp.maximum(c - 1, 0)
    return pl.pallas_call(
        _swa_kernel,
        name="swa_attention",
        grid=(bsz, nq),
        in_specs=[pl.BlockSpec((1, SWA_BLOCK, ATT_WIDTH), lambda b, c: (b, c, 0)),
                  pl.BlockSpec((1, SWA_BLOCK, 2 * KV_WIDTH), lambda b, c: (b, prev(c), 0)),
                  pl.BlockSpec((1, SWA_BLOCK, 2 * KV_WIDTH), lambda b, c: (b, c, 0)),
                  pl.BlockSpec((N_KV_HEADS, 2 * SWA_BLOCK, gq), lambda b, c: (0, 0, 0)),
                  pl.BlockSpec((N_KV_HEADS, 1, gq), lambda b, c: (0, 0, 0))],
        out_specs=pl.BlockSpec((1, SWA_BLOCK, ATT_WIDTH), lambda b, c: (b, c, 0)),
        out_shape=jax.ShapeDtypeStruct((bsz, s, ATT_WIDTH), jnp.bfloat16),
        compiler_params=pltpu.CompilerParams(
            dimension_semantics=("parallel", "arbitrary"), vmem_limit_bytes=VMEM_LIMIT),
    )(q, kv, kv, band_bias, sinks)


def _swa_band_bias(rel_bias):
    qi = jnp.arange(SWA_BLOCK)[:, None]
    kj = jnp.arange(2 * SWA_BLOCK)[None, :]
    dist = qi + SWA_BLOCK - kj
    in_band = (dist >= 0) & (dist < SWA_WINDOW)
    bias = jnp.where(in_band, _bias_by_bucket(rel_bias, _t5_bucket(dist)) * LOG2E, NEG)
    bias = bias.reshape(N_KV_HEADS, Q_GROUP, SWA_BLOCK, 2 * SWA_BLOCK)
    return bias.transpose(0, 3, 1, 2).reshape(N_KV_HEADS, 2 * SWA_BLOCK, Q_GROUP * SWA_BLOCK)


def _swa_sink_rows(sinks):
    rows = jnp.repeat(sinks.astype(jnp.float32) * LOG2E, SWA_BLOCK)
    return rows.reshape(N_KV_HEADS, 1, Q_GROUP * SWA_BLOCK)


def _pad_ffn_weights(w_up, conv_w, conv_b, w_down):
    pad = D_FF_PAD - D_FF

    def halves(a, dtype):
        return tuple(jnp.pad(part.astype(dtype), ((0, 0), (0, 0), (0, pad)))
                     for part in (a[..., :D_FF], a[..., D_FF:]))

    return (halves(w_up, jnp.bfloat16) + halves(conv_w, jnp.float32) + halves(conv_b[:, None, :], jnp.float32)
            + (jnp.pad(w_down.astype(jnp.bfloat16), ((0, 0), (0, pad), (0, 0))),))


def kernel(x, rel_bias, moba_w_qkv, moba_w_o, swa_w_kv, swa_w_q, swa_sinks, swa_w_o,
           ffn_w_up, ffn_conv_w, ffn_conv_b, ffn_w_down, ln_g, ln_b):
    bsz, s, d = x.shape
    bf16 = jnp.bfloat16
    moba_bias = _moba_bias_tables(rel_bias)
    band_bias = _swa_band_bias(rel_bias)
    w_qkv, w_o_moba = moba_w_qkv.astype(bf16), moba_w_o.astype(bf16)
    w_kv, w_q, w_o_swa = swa_w_kv.astype(bf16)[None], swa_w_q.astype(bf16), swa_w_o.astype(bf16)
    ffn_weights = _pad_ffn_weights(ffn_w_up, ffn_conv_w, ffn_conv_b, ffn_w_down)

    h = x.reshape(bsz * s, d)
    kv = None
    for layer in range(DEPTH):
        if layer < N_A_LAYERS:
            qkv = _proj(h, w_qkv, layer, scaled_cols=ATT_WIDTH, scale=SCALE * LOG2E,
                        group=(bsz, s, MOBA_HEADS_PER_STEP * HEAD_DIM))
            attn = _moba_attention(qkv, moba_bias)
            w_o, j = w_o_moba, layer
        else:
            j = layer - N_A_LAYERS
            if kv is None:
                kv = _proj(h, w_kv, 0).reshape(bsz, s, 2 * KV_WIDTH)
            q = _proj(h, w_q, j, scaled_cols=ATT_WIDTH, scale=SCALE * LOG2E).reshape(bsz, s, ATT_WIDTH)
            attn = _swa_attention(q, kv, band_bias, _swa_sink_rows(swa_sinks[j]))
            w_o = w_o_swa
        h = _proj_ln(attn.reshape(bsz * s, ATT_WIDTH), w_o, j, h,
                     ln_g[layer, 0][None, :], ln_b[layer, 0][None, :])
        h = _conv_ffn_ln(h, ffn_weights, layer,
                         ln_g[layer, 1][None, :], ln_b[layer, 1][None, :], s)
    return h.reshape(bsz, s, d)
```

```python
import functools
import math

import jax
import jax.numpy as jnp
from jax import lax
from jax.experimental import pallas as pl
from jax.experimental.pallas import tpu as pltpu

D_MODEL = 2048
N_HEADS = 16
HEAD_DIM = 128
N_KV_HEADS = 4
Q_GROUP = N_HEADS // N_KV_HEADS
ATT_WIDTH = N_HEADS * HEAD_DIM
KV_WIDTH = N_KV_HEADS * HEAD_DIM
MOBA_BLOCK = 256
MOBA_TOPK = 3
SWA_WINDOW = 128
SWA_BLOCK = 128
NUM_BUCKETS = 32
MAX_DISTANCE = 128
D_FF = 5504
CONV_WIDTH = 3
DEPTH = 4
N_A_LAYERS = DEPTH // 2
ALPHA = (2.0 * DEPTH) ** 0.25
LN_EPS = 1e-5
NEG = -1e30
BIG = 1e30
SCALE = HEAD_DIM ** -0.5
LOG2E = math.log2(math.e)

V7X_VMEM_BYTES = 64 * 1024 * 1024
VMEM_LIMIT = 56 * 1024 * 1024
SUBLANES = 8
FFN_PIECE = 256
FF_TILE = 4 * FFN_PIECE
D_FF_PAD = -(-D_FF // FF_TILE) * FF_TILE
FFN_ROWS = 512
PROJ_ROWS = 1024
PROJ_COLS = 1024
LN_ROWS = 512
MOBA_HEADS_PER_STEP = 4

_NT = (((1,), (1,)), ((), ()))


def _t5_bucket(dist):
    n = jnp.maximum(dist, 0)
    max_exact = NUM_BUCKETS // 2
    nf = jnp.maximum(n, 1).astype(jnp.float32)
    large = max_exact + (jnp.log(nf / max_exact) / math.log(MAX_DISTANCE / max_exact)
                         * (NUM_BUCKETS - max_exact)).astype(jnp.int32)
    large = jnp.minimum(large, NUM_BUCKETS - 1)
    return jnp.where(n < max_exact, n, large)


def _layer_norm(y, g, b):
    mu = jnp.mean(y, axis=-1, keepdims=True)
    yc = y - mu
    var = jnp.mean(yc * yc, axis=-1, keepdims=True)
    return yc * lax.rsqrt(var + LN_EPS) * g + b


def _proj_kernel(h_ref, w_ref, o_ref, xb_ref, *, scaled_tiles, scale):
    j = pl.program_id(1)

    @pl.when(j == 0)
    def _():
        xb_ref[...] = h_ref[...].astype(jnp.bfloat16)

    r = jnp.dot(xb_ref[...], w_ref[...], preferred_element_type=jnp.float32)
    if scaled_tiles:
        r = r * jnp.where(j < scaled_tiles, scale, 1.0)
    if len(o_ref.shape) == 3:
        gw = o_ref.shape[2]
        for k in range(o_ref.shape[0]):
            o_ref[k] = r[:, k * gw:(k + 1) * gw].astype(o_ref.dtype)
    else:
        o_ref[...] = r.astype(o_ref.dtype)


def _proj(h, w, layer, scaled_cols=0, scale=1.0, group=None):
    t, d = h.shape
    n = w.shape[2]
    tm, tn = PROJ_ROWS, min(PROJ_COLS, n)
    assert scaled_cols % tn == 0
    if group is None:
        out_spec = pl.BlockSpec((tm, tn), lambda i, j: (i, j))
        out_shape = jax.ShapeDtypeStruct((t, n), jnp.bfloat16)
    else:
        bsz, seq, gw = group
        tiles_per_seq = seq // tm
        assert bsz * seq == t and seq % tm == 0 and tn % gw == 0
        out_spec = pl.BlockSpec((None, tn // gw, tm, gw),
                                lambda i, j: (i // tiles_per_seq, j, i % tiles_per_seq, 0))
        out_shape = jax.ShapeDtypeStruct((bsz, n // gw, seq, gw), jnp.bfloat16)
    return pl.pallas_call(
        functools.partial(_proj_kernel, scaled_tiles=scaled_cols // tn, scale=scale),
        name="proj",
        grid=(t // tm, n // tn),
        in_specs=[pl.BlockSpec((tm, d), lambda i, j: (i, 0)),
                  pl.BlockSpec((None, d, tn), lambda i, j: (layer, 0, j))],
        out_specs=out_spec,
        out_shape=out_shape,
        scratch_shapes=[pltpu.VMEM((tm, d), jnp.bfloat16)],
        compiler_params=pltpu.CompilerParams(
            dimension_semantics=("parallel", "arbitrary"), vmem_limit_bytes=VMEM_LIMIT),
    )(h, w)


def _proj_ln_kernel(a_ref, w_ref, h_ref, g_ref, b_ref, o_ref):
    half = a_ref.shape[0] // 2
    for rows in (slice(0, half), slice(half, 2 * half)):
        y = ALPHA * h_ref[rows, :] + jnp.dot(a_ref[rows, :], w_ref[...], preferred_element_type=jnp.float32)
        o_ref[rows, :] = _layer_norm(y, g_ref[...], b_ref[...])


def _proj_ln(a, w, layer, h, g, b):
    t, k = a.shape
    d = w.shape[2]
    tm = LN_ROWS
    return pl.pallas_call(
        _proj_ln_kernel,
        name="proj_ln",
        grid=(t // tm,),
        in_specs=[pl.BlockSpec((tm, k), lambda i: (i, 0)),
                  pl.BlockSpec((None, k, d), lambda i: (layer, 0, 0)),
                  pl.BlockSpec((tm, d), lambda i: (i, 0)),
                  pl.BlockSpec((1, d), lambda i: (0, 0)),
                  pl.BlockSpec((1, d), lambda i: (0, 0))],
        out_specs=pl.BlockSpec((tm, d), lambda i: (i, 0)),
        out_shape=jax.ShapeDtypeStruct((t, d), jnp.float32),
        compiler_params=pltpu.CompilerParams(
            dimension_semantics=("parallel",), vmem_limit_bytes=VMEM_LIMIT),
    )(a, w, h, g, b)


def _ffn_kernel(h_ref, halo_ref, wg_ref, wv_ref, cwg_ref, cwv_ref, cbg_ref, cbv_ref, wd_ref,
                g_ref, b_ref, o_ref, x_ref, acc_ref, *u_refs, seq_len, last_pieces):
    i = pl.program_id(0)
    f = pl.program_id(1)
    tm = h_ref.shape[0]
    last = pl.num_programs(1) - 1

    @pl.when(f == 0)
    def _():
        at_start = (i * tm) % seq_len == 0
        halo = jnp.where(at_start, 0.0, halo_ref[...])
        x_ref[0:SUBLANES, :] = halo.astype(jnp.bfloat16)
        x_ref[SUBLANES:, :] = h_ref[...].astype(jnp.bfloat16)
        acc_ref[...] = ALPHA * h_ref[...]

    def conv(u, cw_ref, cb_ref, cs):
        u1 = pltpu.roll(u, 1, 0)
        u2 = pltpu.roll(u, 2, 0)
        return (u2[SUBLANES:] * cw_ref[0:1, cs] + u1[SUBLANES:] * cw_ref[1:2, cs]
                + u[SUBLANES:] * cw_ref[2:3, cs] + cb_ref[:, cs])

    def pieces(n):
        x = x_ref[...]
        cols = [slice(k * FFN_PIECE, (k + 1) * FFN_PIECE) for k in range(n)]
        base = jnp.minimum(f, 0)
        for u_ref, cs in zip(u_refs, cols):
            u_ref[base] = jnp.dot(x, wg_ref[:, cs], preferred_element_type=jnp.float32)
            u_ref[base + 1] = jnp.dot(x, wv_ref[:, cs], preferred_element_type=jnp.float32)
        acts = []
        for u_ref, cs in zip(u_refs, cols):
            gate = conv(u_ref[0], cwg_ref, cbg_ref, cs)
            val = conv(u_ref[1], cwv_ref, cbv_ref, cs)
            acts.append((jax.nn.gelu(gate) * val).astype(jnp.bfloat16))
        for act, cs in zip(acts, cols):
            acc_ref[...] += jnp.dot(act, wd_ref[cs, :], preferred_element_type=jnp.float32)

    @pl.when(f < last)
    def _():
        pieces(len(u_refs))

    @pl.when(f == last)
    def _():
        pieces(last_pieces)
        o_ref[...] = _layer_norm(acc_ref[...], g_ref[...], b_ref[...])


def _conv_ffn_ln(h, ffn_weights, layer, g, b, seq_len):
    t, d = h.shape
    tm, tf = FFN_ROWS, FF_TILE
    nf = D_FF_PAD // tf
    pieces_per_step = tf // FFN_PIECE
    last_pieces = -(-D_FF // FFN_PIECE) - (nf - 1) * pieces_per_step
    halo_blocks = tm // SUBLANES
    return pl.pallas_call(
        functools.partial(_ffn_kernel, seq_len=seq_len, last_pieces=last_pieces),
        name="conv_ffn_ln",
        grid=(t // tm, nf),
        in_specs=[pl.BlockSpec((tm, d), lambda i, f: (i, 0)),
                  pl.BlockSpec((SUBLANES, d), lambda i, f: (jnp.maximum(i * halo_blocks - 1, 0), 0)),
                  pl.BlockSpec((None, d, tf), lambda i, f: (layer, 0, f)),
                  pl.BlockSpec((None, d, tf), lambda i, f: (layer, 0, f)),
                  pl.BlockSpec((None, CONV_WIDTH, tf), lambda i, f: (layer, 0, f)),
                  pl.BlockSpec((None, CONV_WIDTH, tf), lambda i, f: (layer, 0, f)),
                  pl.BlockSpec((None, 1, tf), lambda i, f: (layer, 0, f)),
                  pl.BlockSpec((None, 1, tf), lambda i, f: (layer, 0, f)),
                  pl.BlockSpec((None, tf, d), lambda i, f: (layer, f, 0)),
                  pl.BlockSpec((1, d), lambda i, f: (0, 0)),
                  pl.BlockSpec((1, d), lambda i, f: (0, 0))],
        out_specs=pl.BlockSpec((tm, d), lambda i, f: (i, 0)),
        out_shape=jax.ShapeDtypeStruct((t, d), jnp.float32),
        scratch_shapes=[pltpu.VMEM((tm + SUBLANES, d), jnp.bfloat16),
                        pltpu.VMEM((tm, d), jnp.float32)]
        + [pltpu.VMEM((2, tm + SUBLANES, FFN_PIECE), jnp.float32)] * pieces_per_step,
        compiler_params=pltpu.CompilerParams(
            dimension_semantics=("parallel", "arbitrary"), vmem_limit_bytes=VMEM_LIMIT),
    )(h, h, *ffn_weights, g, b)


def _moba_kernel(q_ref, k_ref, v_ref, bias_ref, o_ref, *scratch):
    _moba_prepare(k_ref, v_ref, *scratch[:3])

    def body(c, carry):
        _moba_query_block(c, q_ref, k_ref, bias_ref, o_ref, *scratch)
        return carry

    lax.fori_loop(0, q_ref.shape[0] // MOBA_BLOCK, body, 0)


def _moba_prepare(k_ref, v_ref, kmean_ref, kparts_ref, vt_ref):
    n_heads, n_blocks = kmean_ref.shape[0], kmean_ref.shape[1]
    blk = MOBA_BLOCK
    cols = lambda j: slice(j * HEAD_DIM, (j + 1) * HEAD_DIM)
    rows = lax.broadcasted_iota(jnp.int32, (HEAD_DIM, HEAD_DIM), 0)
    lanes = lax.broadcasted_iota(jnp.int32, (HEAD_DIM, HEAD_DIM), 1)
    eye = (rows == lanes).astype(jnp.bfloat16)

    unroll = 8
    assert n_blocks % unroll == 0

    def build(i, carry):
        for n in [i * unroll + u for u in range(unroll)]:
            keys = pl.ds(pl.multiple_of(n * blk, blk), blk)
            for j in range(n_heads):
                kb = k_ref[keys, cols(j)]
                vb = v_ref[keys, cols(j)]
                kmean_ref[j, pl.ds(n, 1), :] = (
                    jnp.sum(kb.astype(jnp.float32), axis=0, keepdims=True) * (1.0 / blk))
                vt_ref[j, n] = lax.dot_general(eye, vb, _NT,
                                               preferred_element_type=jnp.float32).astype(jnp.bfloat16)
        return carry

    lax.fori_loop(0, n_blocks // unroll, build, 0)
    for j in range(n_heads):
        km = kmean_ref[j]
        hi = km.astype(jnp.bfloat16)
        r1 = km - hi.astype(jnp.float32)
        mid = r1.astype(jnp.bfloat16)
        lo = (r1 - mid.astype(jnp.float32)).astype(jnp.bfloat16)
        kparts_ref[j] = jnp.concatenate([hi, mid, lo], axis=0)


def _moba_query_block(c, q_ref, k_ref, bias_ref, o_ref,
                      kmean_ref, kparts_ref, vt_ref, sel_ref, m_ref, l_ref, acc_ref, sa_ref, sb_ref):
    n_heads, n_blocks = kmean_ref.shape[0], kmean_ref.shape[1]
    blk = MOBA_BLOCK
    cols = lambda j: slice(j * HEAD_DIM, (j + 1) * HEAD_DIM)
    qrows = pl.ds(pl.multiple_of(c * blk, blk), blk)

    qs = [q_ref[qrows, cols(j)] for j in range(n_heads)]

    for j in range(n_heads):
        g3 = lax.dot_general(kparts_ref[j], qs[j], _NT, preferred_element_type=jnp.float32)
        gate = (g3[2 * n_blocks:] + g3[n_blocks:2 * n_blocks]) + g3[:n_blocks]
        row = lax.broadcasted_iota(jnp.int32, gate.shape, 0)
        past = row < c
        gate = jnp.where(past, gate, NEG)
        rank = jnp.zeros(gate.shape, jnp.int32)
        for m in range(n_blocks):
            gm = gate[m:m + 1, :]
            beats = (gm > gate) | ((gm == gate) & (row > m))
            rank = rank + beats.astype(jnp.int32)
        sel_ref[j] = (((rank < MOBA_TOPK) & past) | (row == c)).astype(jnp.float32)
        m_ref[j] = jnp.full(m_ref.shape[1:], NEG, jnp.float32)
        l_ref[j] = jnp.zeros(l_ref.shape[1:], jnp.float32)
        acc_ref[j] = jnp.zeros(acc_ref.shape[1:], jnp.float32)

    def score_matmuls(dst_ref, n):
        keys = pl.ds(pl.multiple_of(jnp.minimum(n, c) * blk, blk), blk)
        for j in range(n_heads):
            dst_ref[j] = lax.dot_general(k_ref[keys, cols(j)], qs[j], _NT,
                                         preferred_element_type=jnp.float32)

    def softmax_update(src_ref, n, far):
        live = n <= c
        nb_ = jnp.minimum(n, c)
        kind = jnp.clip(n - (c - 2), 0, 2)
        ps, alphas = [], []
        for j in range(n_heads):
            if far:
                s = src_ref[j]
                shift = bias_ref[j, 0, 0:1, :]
            else:
                s = src_ref[j] + bias_ref[j, kind]
                shift = 0.0
            chosen = jnp.logical_and(sel_ref[j, pl.ds(nb_, 1), :] > 0.5, live)
            m_old = m_ref[j]
            m_new = jnp.where(chosen, jnp.maximum(m_old, jnp.max(s, axis=0, keepdims=True) + shift), m_old)
            p = jnp.exp2(s - jnp.where(chosen, m_new - shift, BIG))
            alphas.append(jnp.exp2(m_old - m_new))
            m_ref[j] = m_new
            l_ref[j] = alphas[j] * l_ref[j] + jnp.sum(p, axis=0, keepdims=True)
            ps.append(p.astype(jnp.bfloat16))
        for j in range(n_heads):
            acc_ref[j] = alphas[j] * acc_ref[j] + jnp.dot(
                vt_ref[j, nb_], ps[j], preferred_element_type=jnp.float32)

    score_matmuls(sa_ref, 0)

    def far_pair(i, carry):
        n = 2 * i
        score_matmuls(sb_ref, n + 1)
        softmax_update(sa_ref, n, True)
        score_matmuls(sa_ref, n + 2)
        softmax_update(sb_ref, n + 1, True)
        return carry

    def last_pair(first_ref, second_ref, n):
        score_matmuls(second_ref, n + 1)
        softmax_update(first_ref, n, False)
        softmax_update(second_ref, n + 1, False)

    def far_quad(q, carry):
        far_pair(2 * q, carry)
        return far_pair(2 * q + 1, carry)

    n_far = jnp.maximum(c - 1, 0)
    far_pairs = n_far // 2
    far_quads = far_pairs // 2
    lax.fori_loop(0, far_quads, far_quad, 0)
    lax.fori_loop(2 * far_quads, far_pairs, far_pair, 0)
    n0 = 2 * far_pairs
    odd_far = n_far % 2 == 1

    @pl.when(jnp.logical_not(odd_far))
    def _():
        last_pair(sa_ref, sb_ref, n0)

    @pl.when(odd_far)
    def _():
        score_matmuls(sb_ref, n0 + 1)
        softmax_update(sa_ref, n0, True)
        last_pair(sb_ref, sa_ref, n0 + 1)

    for j in range(n_heads):
        out_t = acc_ref[j] / l_ref[j]
        o_ref[0, qrows, cols(j)] = out_t.T.astype(o_ref.dtype)


def _moba_attention(qkv, bias_tiles):
    hpb = MOBA_HEADS_PER_STEP
    ng = N_HEADS // hpb
    gw = hpb * HEAD_DIM
    bsz, _, s, _ = qkv.shape
    assert qkv.shape == (bsz, 3 * ng, s, gw)
    nb = s // MOBA_BLOCK
    return pl.pallas_call(
        _moba_kernel,
        name="moba_attention",
        grid=(bsz, ng),
        in_specs=[pl.BlockSpec((None, None, s, gw), lambda b, g: (b, g, 0, 0)),
                  pl.BlockSpec((None, None, s, gw), lambda b, g: (b, ng + g, 0, 0)),
                  pl.BlockSpec((None, None, s, gw), lambda b, g: (b, 2 * ng + g, 0, 0)),
                  pl.BlockSpec((hpb, 3, MOBA_BLOCK, MOBA_BLOCK), lambda b, g: (g, 0, 0, 0))],
        out_specs=pl.BlockSpec((1, s, gw), lambda b, g: (b, 0, g)),
        out_shape=jax.ShapeDtypeStruct((bsz, s, ATT_WIDTH), jnp.bfloat16),
        scratch_shapes=[pltpu.VMEM((hpb, nb, HEAD_DIM), jnp.float32),
                        pltpu.VMEM((hpb, 3 * nb, HEAD_DIM), jnp.bfloat16),
                        pltpu.VMEM((hpb, nb, HEAD_DIM, MOBA_BLOCK), jnp.bfloat16),
                        pltpu.VMEM((hpb, nb, MOBA_BLOCK), jnp.float32),
                        pltpu.VMEM((hpb, 1, MOBA_BLOCK), jnp.float32),
                        pltpu.VMEM((hpb, 1, MOBA_BLOCK), jnp.float32),
                        pltpu.VMEM((hpb, HEAD_DIM, MOBA_BLOCK), jnp.float32),
                        pltpu.VMEM((hpb, MOBA_BLOCK, MOBA_BLOCK), jnp.float32),
                        pltpu.VMEM((hpb, MOBA_BLOCK, MOBA_BLOCK), jnp.float32)],
        compiler_params=pltpu.CompilerParams(
            dimension_semantics=("parallel", "parallel"), vmem_limit_bytes=VMEM_LIMIT),
    )(qkv, qkv, qkv, bias_tiles)


def _bias_by_bucket(rel_bias, bucket):
    onehot = (bucket[None] == jnp.arange(NUM_BUCKETS).reshape((NUM_BUCKETS,) + (1,) * bucket.ndim))
    return jnp.einsum('kh,k...->h...', rel_bias.astype(jnp.float32), onehot.astype(jnp.float32),
                      precision=lax.Precision.HIGHEST)


def _moba_bias_tables(rel_bias):
    key = jnp.arange(MOBA_BLOCK)[:, None]
    qry = jnp.arange(MOBA_BLOCK)[None, :]
    d_own = qry - key
    own = jnp.where(d_own >= 0, _bias_by_bucket(rel_bias, _t5_bucket(d_own)), NEG)
    prev = _bias_by_bucket(rel_bias, _t5_bucket(d_own + MOBA_BLOCK))
    far = jnp.broadcast_to(rel_bias.astype(jnp.float32)[NUM_BUCKETS - 1][:, None, None], own.shape)
    return jnp.stack([far, prev, own], axis=1) * LOG2E


def _swa_kernel(q_ref, kvp_ref, kvc_ref, bias_ref, sink_ref, o_ref):
    c = pl.program_id(1)
    has_prev = c > 0
    blk = SWA_BLOCK
    rows = lax.broadcasted_iota(jnp.int32, (HEAD_DIM, HEAD_DIM), 0)
    lanes = lax.broadcasted_iota(jnp.int32, (HEAD_DIM, HEAD_DIM), 1)
    eye = (rows == lanes).astype(jnp.bfloat16)
    head = lambda hh: slice(hh * HEAD_DIM, (hh + 1) * HEAD_DIM)
    kv_cols = lambda g, is_v: slice((is_v * N_KV_HEADS + g) * HEAD_DIM, (is_v * N_KV_HEADS + g + 1) * HEAD_DIM)

    scores, values = [], []
    for g in range(N_KV_HEADS):
        qg = jnp.concatenate([q_ref[0, :, head(g * Q_GROUP + i)] for i in range(Q_GROUP)], axis=0)
        scores.append(tuple(
            lax.dot_general(ref[0, :, kv_cols(g, 0)], qg, _NT, preferred_element_type=jnp.float32)
            for ref in (kvp_ref, kvc_ref)))
        values.append(tuple(
            lax.dot_general(eye, ref[0, :, kv_cols(g, 1)], _NT,
                            preferred_element_type=jnp.float32).astype(jnp.bfloat16)
            for ref in (kvp_ref, kvc_ref)))

    probs = []
    for g in range(N_KV_HEADS):
        s_prev = jnp.where(has_prev, scores[g][0] + bias_ref[g, :blk, :], NEG)
        s_cur = scores[g][1] + bias_ref[g, blk:, :]
        sink = sink_ref[g]
        m = jnp.maximum(jnp.maximum(jnp.max(s_prev, axis=0, keepdims=True),
                                    jnp.max(s_cur, axis=0, keepdims=True)), sink)
        p_prev = jnp.exp2(s_prev - m)
        p_cur = jnp.exp2(s_cur - m)
        denom = (jnp.sum(p_prev, axis=0, keepdims=True) + jnp.sum(p_cur, axis=0, keepdims=True)
                 + jnp.exp2(sink - m))
        probs.append((p_prev.astype(jnp.bfloat16), p_cur.astype(jnp.bfloat16), denom))

    for g in range(N_KV_HEADS):
        p_prev, p_cur, denom = probs[g]
        out_t = (jnp.dot(values[g][0], p_prev, preferred_element_type=jnp.float32)
                 + jnp.dot(values[g][1], p_cur, preferred_element_type=jnp.float32)) / denom
        for i in range(Q_GROUP):
            o_ref[0, :, head(g * Q_GROUP + i)] = out_t[:, i * blk:(i + 1) * blk].T.astype(o_ref.dtype)


def _swa_attention(q, kv, band_bias, sinks):
    bsz, s, _ = q.shape
    nq = s // SWA_BLOCK
    gq = Q_GROUP * SWA_BLOCK
    prev = lambda c: jnp.maximum(c - 1, 0)
    return pl.pallas_call(
        _swa_kernel,
        name="swa_attention",
        grid=(bsz, nq),
        in_specs=[pl.BlockSpec((1, SWA_BLOCK, ATT_WIDTH), lambda b, c: (b, c, 0)),
                  pl.BlockSpec((1, SWA_BLOCK, 2 * KV_WIDTH), lambda b, c: (b, prev(c), 0)),
                  pl.BlockSpec((1, SWA_BLOCK, 2 * KV_WIDTH), lambda b, c: (b, c, 0)),
                  pl.BlockSpec((N_KV_HEADS, 2 * SWA_BLOCK, gq), lambda b, c: (0, 0, 0)),
                  pl.BlockSpec((N_KV_HEADS, 1, gq), lambda b, c: (0, 0, 0))],
        out_specs=pl.BlockSpec((1, SWA_BLOCK, ATT_WIDTH), lambda b, c: (b, c, 0)),
        out_shape=jax.ShapeDtypeStruct((bsz, s, ATT_WIDTH), jnp.bfloat16),
        compiler_params=pltpu.CompilerParams(
            dimension_semantics=("parallel", "arbitrary"), vmem_limit_bytes=VMEM_LIMIT),
    )(q, kv, kv, band_bias, sinks)


def _swa_band_bias(rel_bias):
    qi = jnp.arange(SWA_BLOCK)[:, None]
    kj = jnp.arange(2 * SWA_BLOCK)[None, :]
    dist = qi + SWA_BLOCK - kj
    in_band = (dist >= 0) & (dist < SWA_WINDOW)
    bias = jnp.where(in_band, _bias_by_bucket(rel_bias, _t5_bucket(dist)) * LOG2E, NEG)
    bias = bias.reshape(N_KV_HEADS, Q_GROUP, SWA_BLOCK, 2 * SWA_BLOCK)
    return bias.transpose(0, 3, 1, 2).reshape(N_KV_HEADS, 2 * SWA_BLOCK, Q_GROUP * SWA_BLOCK)


def _swa_sink_rows(sinks):
    rows = jnp.repeat(sinks.astype(jnp.float32) * LOG2E, SWA_BLOCK)
    return rows.reshape(N_KV_HEADS, 1, Q_GROUP * SWA_BLOCK)


def _pad_ffn_weights(w_up, conv_w, conv_b, w_down):
    pad = D_FF_PAD - D_FF

    def halves(a, dtype):
        return tuple(jnp.pad(part.astype(dtype), ((0, 0), (0, 0), (0, pad)))
                     for part in (a[..., :D_FF], a[..., D_FF:]))

    return (halves(w_up, jnp.bfloat16) + halves(conv_w, jnp.float32) + halves(conv_b[:, None, :], jnp.float32)
            + (jnp.pad(w_down.astype(jnp.bfloat16), ((0, 0), (0, pad), (0, 0))),))


def kernel(x, rel_bias, moba_w_qkv, moba_w_o, swa_w_kv, swa_w_q, swa_sinks, swa_w_o,
           ffn_w_up, ffn_conv_w, ffn_conv_b, ffn_w_down, ln_g, ln_b):
    bsz, s, d = x.shape
    bf16 = jnp.bfloat16
    moba_bias = _moba_bias_tables(rel_bias)
    band_bias = _swa_band_bias(rel_bias)
    w_qkv, w_o_moba = moba_w_qkv.astype(bf16), moba_w_o.astype(bf16)
    w_kv, w_q, w_o_swa = swa_w_kv.astype(bf16)[None], swa_w_q.astype(bf16), swa_w_o.astype(bf16)
    ffn_weights = _pad_ffn_weights(ffn_w_up, ffn_conv_w, ffn_conv_b, ffn_w_down)

    h = x.reshape(bsz * s, d)
    kv = None
    for layer in range(DEPTH):
        if layer < N_A_LAYERS:
            qkv = _proj(h, w_qkv, layer, scaled_cols=ATT_WIDTH, scale=SCALE * LOG2E,
                        group=(bsz, s, MOBA_HEADS_PER_STEP * HEAD_DIM))
            attn = _moba_attention(qkv, moba_bias)
            w_o, j = w_o_moba, layer
        else:
            j = layer - N_A_LAYERS
            if kv is None:
                kv = _proj(h, w_kv, 0).reshape(bsz, s, 2 * KV_WIDTH)
            q = _proj(h, w_q, j, scaled_cols=ATT_WIDTH, scale=SCALE * LOG2E).reshape(bsz, s, ATT_WIDTH)
            attn = _swa_attention(q, kv, band_bias, _swa_sink_rows(swa_sinks[j]))
            w_o = w_o_swa
        h = _proj_ln(attn.reshape(bsz * s, ATT_WIDTH), w_o, j, h,
                     ln_g[layer, 0][None, :], ln_b[layer, 0][None, :])
        h = _conv_ffn_ln(h, ffn_weights, layer,
                         ln_g[layer, 1][None, :], ln_b[layer, 1][None, :], s)
    return h.reshape(bsz, s, d)
```

```python
import functools
import math

import jax
import jax.numpy as jnp
from jax import lax
from jax.experimental import pallas as pl
from jax.experimental.pallas import tpu as pltpu

D_MODEL = 2048
N_HEADS = 16
HEAD_DIM = 128
N_KV_HEADS = 4
Q_GROUP = N_HEADS // N_KV_HEADS
ATT_WIDTH = N_HEADS * HEAD_DIM
KV_WIDTH = N_KV_HEADS * HEAD_DIM
MOBA_BLOCK = 256
MOBA_TOPK = 3
SWA_WINDOW = 128
SWA_BLOCK = 128
NUM_BUCKETS = 32
MAX_DISTANCE = 128
D_FF = 5504
CONV_WIDTH = 3
DEPTH = 4
N_A_LAYERS = DEPTH // 2
ALPHA = (2.0 * DEPTH) ** 0.25
LN_EPS = 1e-5
NEG = -1e30
BIG = 1e30
SCALE = HEAD_DIM ** -0.5
LOG2E = math.log2(math.e)

V7X_VMEM_BYTES = 64 * 1024 * 1024
VMEM_LIMIT = 56 * 1024 * 1024
SUBLANES = 8
FFN_PIECE = 256
FF_TILE = 4 * FFN_PIECE
D_FF_PAD = -(-D_FF // FF_TILE) * FF_TILE
FFN_ROWS = 512
PROJ_ROWS = 1024
PROJ_COLS = 1024
LN_ROWS = 512
MOBA_HEADS_PER_STEP = 4
SWA_BLOCKS_PER_STEP = 2

_NT = (((1,), (1,)), ((), ()))


def _t5_bucket(dist):
    n = jnp.maximum(dist, 0)
    max_exact = NUM_BUCKETS // 2
    nf = jnp.maximum(n, 1).astype(jnp.float32)
    large = max_exact + (jnp.log(nf / max_exact) / math.log(MAX_DISTANCE / max_exact)
                         * (NUM_BUCKETS - max_exact)).astype(jnp.int32)
    large = jnp.minimum(large, NUM_BUCKETS - 1)
    return jnp.where(n < max_exact, n, large)


def _layer_norm(y, g, b):
    mu = jnp.mean(y, axis=-1, keepdims=True)
    yc = y - mu
    var = jnp.mean(yc * yc, axis=-1, keepdims=True)
    return yc * lax.rsqrt(var + LN_EPS) * g + b


def _proj_kernel(h_ref, w_ref, o_ref, xb_ref, *, scaled_tiles, scale):
    j = pl.program_id(1)

    @pl.when(j == 0)
    def _():
        xb_ref[...] = h_ref[...].astype(jnp.bfloat16)

    r = jnp.dot(xb_ref[...], w_ref[...], preferred_element_type=jnp.float32)
    if scaled_tiles:
        r = r * jnp.where(j < scaled_tiles, scale, 1.0)
    if len(o_ref.shape) == 3:
        gw = o_ref.shape[2]
        for k in range(o_ref.shape[0]):
            o_ref[k] = r[:, k * gw:(k + 1) * gw].astype(o_ref.dtype)
    else:
        o_ref[...] = r.astype(o_ref.dtype)


def _proj(h, w, layer, scaled_cols=0, scale=1.0, group=None):
    t, d = h.shape
    n = w.shape[2]
    tm, tn = PROJ_ROWS, min(PROJ_COLS, n)
    assert scaled_cols % tn == 0
    if group is None:
        out_spec = pl.BlockSpec((tm, tn), lambda i, j: (i, j))
        out_shape = jax.ShapeDtypeStruct((t, n), jnp.bfloat16)
    else:
        bsz, seq, gw = group
        tiles_per_seq = seq // tm
        assert bsz * seq == t and seq % tm == 0 and tn % gw == 0
        out_spec = pl.BlockSpec((None, tn // gw, tm, gw),
                                lambda i, j: (i // tiles_per_seq, j, i % tiles_per_seq, 0))
        out_shape = jax.ShapeDtypeStruct((bsz, n // gw, seq, gw), jnp.bfloat16)
    return pl.pallas_call(
        functools.partial(_proj_kernel, scaled_tiles=scaled_cols // tn, scale=scale),
        name="proj",
        grid=(t // tm, n // tn),
        in_specs=[pl.BlockSpec((tm, d), lambda i, j: (i, 0)),
                  pl.BlockSpec((None, d, tn), lambda i, j: (layer, 0, j))],
        out_specs=out_spec,
        out_shape=out_shape,
        scratch_shapes=[pltpu.VMEM((tm, d), jnp.bfloat16)],
        compiler_params=pltpu.CompilerParams(
            dimension_semantics=("parallel", "arbitrary"), vmem_limit_bytes=VMEM_LIMIT),
    )(h, w)


def _proj_ln_kernel(a_ref, w_ref, h_ref, g_ref, b_ref, o_ref):
    half = a_ref.shape[0] // 2
    for rows in (slice(0, half), slice(half, 2 * half)):
        y = ALPHA * h_ref[rows, :] + jnp.dot(a_ref[rows, :], w_ref[...], preferred_element_type=jnp.float32)
        o_ref[rows, :] = _layer_norm(y, g_ref[...], b_ref[...])


def _proj_ln(a, w, layer, h, g, b):
    t, k = a.shape
    d = w.shape[2]
    tm = LN_ROWS
    return pl.pallas_call(
        _proj_ln_kernel,
        name="proj_ln",
        grid=(t // tm,),
        in_specs=[pl.BlockSpec((tm, k), lambda i: (i, 0)),
                  pl.BlockSpec((None, k, d), lambda i: (layer, 0, 0)),
                  pl.BlockSpec((tm, d), lambda i: (i, 0)),
                  pl.BlockSpec((1, d), lambda i: (0, 0)),
                  pl.BlockSpec((1, d), lambda i: (0, 0))],
        out_specs=pl.BlockSpec((tm, d), lambda i: (i, 0)),
        out_shape=jax.ShapeDtypeStruct((t, d), jnp.float32),
        compiler_params=pltpu.CompilerParams(
            dimension_semantics=("parallel",), vmem_limit_bytes=VMEM_LIMIT),
    )(a, w, h, g, b)


def _ffn_kernel(h_ref, halo_ref, wg_ref, wv_ref, cwg_ref, cwv_ref, cbg_ref, cbv_ref, wd_ref,
                g_ref, b_ref, o_ref, x_ref, acc_ref, *u_refs, seq_len, last_pieces):
    i = pl.program_id(0)
    f = pl.program_id(1)
    tm = h_ref.shape[0]
    last = pl.num_programs(1) - 1

    @pl.when(f == 0)
    def _():
        at_start = (i * tm) % seq_len == 0
        halo = jnp.where(at_start, 0.0, halo_ref[...])
        x_ref[0:SUBLANES, :] = halo.astype(jnp.bfloat16)
        x_ref[SUBLANES:, :] = h_ref[...].astype(jnp.bfloat16)
        acc_ref[...] = ALPHA * h_ref[...]

    def conv(u, cw_ref, cb_ref, cs):
        u1 = pltpu.roll(u, 1, 0)
        u2 = pltpu.roll(u, 2, 0)
        return (u2[SUBLANES:] * cw_ref[0:1, cs] + u1[SUBLANES:] * cw_ref[1:2, cs]
                + u[SUBLANES:] * cw_ref[2:3, cs] + cb_ref[:, cs])

    def pieces(n):
        x = x_ref[...]
        cols = [slice(k * FFN_PIECE, (k + 1) * FFN_PIECE) for k in range(n)]
        base = jnp.minimum(f, 0)
        for u_ref, cs in zip(u_refs, cols):
            u_ref[base] = jnp.dot(x, wg_ref[:, cs], preferred_element_type=jnp.float32)
            u_ref[base + 1] = jnp.dot(x, wv_ref[:, cs], preferred_element_type=jnp.float32)
        acts = []
        for u_ref, cs in zip(u_refs, cols):
            gate = conv(u_ref[0], cwg_ref, cbg_ref, cs)
            val = conv(u_ref[1], cwv_ref, cbv_ref, cs)
            acts.append((jax.nn.gelu(gate) * val).astype(jnp.bfloat16))
        for act, cs in zip(acts, cols):
            acc_ref[...] += jnp.dot(act, wd_ref[cs, :], preferred_element_type=jnp.float32)

    @pl.when(f < last)
    def _():
        pieces(len(u_refs))

    @pl.when(f == last)
    def _():
        pieces(last_pieces)
        o_ref[...] = _layer_norm(acc_ref[...], g_ref[...], b_ref[...])


def _conv_ffn_ln(h, ffn_weights, layer, g, b, seq_len):
    t, d = h.shape
    tm, tf = FFN_ROWS, FF_TILE
    nf = D_FF_PAD // tf
    pieces_per_step = tf // FFN_PIECE
    last_pieces = -(-D_FF // FFN_PIECE) - (nf - 1) * pieces_per_step
    halo_blocks = tm // SUBLANES
    return pl.pallas_call(
        functools.partial(_ffn_kernel, seq_len=seq_len, last_pieces=last_pieces),
        name="conv_ffn_ln",
        grid=(t // tm, nf),
        in_specs=[pl.BlockSpec((tm, d), lambda i, f: (i, 0)),
                  pl.BlockSpec((SUBLANES, d), lambda i, f: (jnp.maximum(i * halo_blocks - 1, 0), 0)),
                  pl.BlockSpec((None, d, tf), lambda i, f: (layer, 0, f)),
                  pl.BlockSpec((None, d, tf), lambda i, f: (layer, 0, f)),
                  pl.BlockSpec((None, CONV_WIDTH, tf), lambda i, f: (layer, 0, f)),
                  pl.BlockSpec((None, CONV_WIDTH, tf), lambda i, f: (layer, 0, f)),
                  pl.BlockSpec((None, 1, tf), lambda i, f: (layer, 0, f)),
                  pl.BlockSpec((None, 1, tf), lambda i, f: (layer, 0, f)),
                  pl.BlockSpec((None, tf, d), lambda i, f: (layer, f, 0)),
                  pl.BlockSpec((1, d), lambda i, f: (0, 0)),
                  pl.BlockSpec((1, d), lambda i, f: (0, 0))],
        out_specs=pl.BlockSpec((tm, d), lambda i, f: (i, 0)),
        out_shape=jax.ShapeDtypeStruct((t, d), jnp.float32),
        scratch_shapes=[pltpu.VMEM((tm + SUBLANES, d), jnp.bfloat16),
                        pltpu.VMEM((tm, d), jnp.float32)]
        + [pltpu.VMEM((2, tm + SUBLANES, FFN_PIECE), jnp.float32)] * pieces_per_step,
        compiler_params=pltpu.CompilerParams(
            dimension_semantics=("parallel", "arbitrary"), vmem_limit_bytes=VMEM_LIMIT),
    )(h, h, *ffn_weights, g, b)


def _moba_kernel(q_ref, k_ref, v_ref, bias_ref, o_ref, *scratch):
    _moba_prepare(k_ref, v_ref, *scratch[:3])

    def body(c, carry):
        _moba_query_block(c, q_ref, k_ref, bias_ref, o_ref, *scratch)
        return carry

    lax.fori_loop(0, q_ref.shape[0] // MOBA_BLOCK, body, 0)


def _moba_prepare(k_ref, v_ref, kmean_ref, kparts_ref, vt_ref):
    n_heads, n_blocks = kmean_ref.shape[0], kmean_ref.shape[1]
    blk = MOBA_BLOCK
    cols = lambda j: slice(j * HEAD_DIM, (j + 1) * HEAD_DIM)
    rows = lax.broadcasted_iota(jnp.int32, (HEAD_DIM, HEAD_DIM), 0)
    lanes = lax.broadcasted_iota(jnp.int32, (HEAD_DIM, HEAD_DIM), 1)
    eye = (rows == lanes).astype(jnp.bfloat16)

    unroll = 8
    assert n_blocks % unroll == 0

    def build(i, carry):
        for n in [i * unroll + u for u in range(unroll)]:
            keys = pl.ds(pl.multiple_of(n * blk, blk), blk)
            for j in range(n_heads):
                kb = k_ref[keys, cols(j)]
                vb = v_ref[keys, cols(j)]
                kmean_ref[j, pl.ds(n, 1), :] = (
                    jnp.sum(kb.astype(jnp.float32), axis=0, keepdims=True) * (1.0 / blk))
                vt_ref[j, n] = lax.dot_general(eye, vb, _NT,
                                               preferred_element_type=jnp.float32).astype(jnp.bfloat16)
        return carry

    lax.fori_loop(0, n_blocks // unroll, build, 0)
    for j in range(n_heads):
        km = kmean_ref[j]
        hi = km.astype(jnp.bfloat16)
        r1 = km - hi.astype(jnp.float32)
        mid = r1.astype(jnp.bfloat16)
        lo = (r1 - mid.astype(jnp.float32)).astype(jnp.bfloat16)
        kparts_ref[j] = jnp.concatenate([hi, mid, lo], axis=0)


def _moba_query_block(c, q_ref, k_ref, bias_ref, o_ref,
                      kmean_ref, kparts_ref, vt_ref, sel_ref, m_ref, l_ref, acc_ref, sa_ref, sb_ref):
    n_heads, n_blocks = kmean_ref.shape[0], kmean_ref.shape[1]
    blk = MOBA_BLOCK
    cols = lambda j: slice(j * HEAD_DIM, (j + 1) * HEAD_DIM)
    qrows = pl.ds(pl.multiple_of(c * blk, blk), blk)

    qs = [q_ref[qrows, cols(j)] for j in range(n_heads)]

    for j in range(n_heads):
        g3 = lax.dot_general(kparts_ref[j], qs[j], _NT, preferred_element_type=jnp.float32)
        gate = (g3[2 * n_blocks:] + g3[n_blocks:2 * n_blocks]) + g3[:n_blocks]
        row = lax.broadcasted_iota(jnp.int32, gate.shape, 0)
        past = row < c
        gate = jnp.where(past, gate, NEG)
        rank = jnp.zeros(gate.shape, jnp.int32)
        for m in range(n_blocks):
            gm = gate[m:m + 1, :]
            beats = (gm > gate) | ((gm == gate) & (row > m))
            rank = rank + beats.astype(jnp.int32)
        sel_ref[j] = (((rank < MOBA_TOPK) & past) | (row == c)).astype(jnp.float32)
        m_ref[j] = jnp.full(m_ref.shape[1:], NEG, jnp.float32)
        l_ref[j] = jnp.zeros(l_ref.shape[1:], jnp.float32)
        acc_ref[j] = jnp.zeros(acc_ref.shape[1:], jnp.float32)

    def score_matmuls(dst_ref, n):
        keys = pl.ds(pl.multiple_of(jnp.minimum(n, c) * blk, blk), blk)
        for j in range(n_heads):
            dst_ref[j] = lax.dot_general(k_ref[keys, cols(j)], qs[j], _NT,
                                         preferred_element_type=jnp.float32)

    def softmax_update(src_ref, n, far):
        live = n <= c
        nb_ = jnp.minimum(n, c)
        kind = jnp.clip(n - (c - 2), 0, 2)
        ps, alphas = [], []
        for j in range(n_heads):
            if far:
                s = src_ref[j]
                shift = bias_ref[j, 0, 0:1, :]
            else:
                s = src_ref[j] + bias_ref[j, kind]
                shift = 0.0
            chosen = jnp.logical_and(sel_ref[j, pl.ds(nb_, 1), :] > 0.5, live)
            m_old = m_ref[j]
            m_new = jnp.where(chosen, jnp.maximum(m_old, jnp.max(s, axis=0, keepdims=True) + shift), m_old)
            p = jnp.exp2(s - jnp.where(chosen, m_new - shift, BIG))
            alphas.append(jnp.exp2(m_old - m_new))
            m_ref[j] = m_new
            l_ref[j] = alphas[j] * l_ref[j] + jnp.sum(p, axis=0, keepdims=True)
            ps.append(p.astype(jnp.bfloat16))
        for j in range(n_heads):
            acc_ref[j] = alphas[j] * acc_ref[j] + jnp.dot(
                vt_ref[j, nb_], ps[j], preferred_element_type=jnp.float32)

    score_matmuls(sa_ref, 0)

    def far_pair(i, carry):
        n = 2 * i
        score_matmuls(sb_ref, n + 1)
        softmax_update(sa_ref, n, True)
        score_matmuls(sa_ref, n + 2)
        softmax_update(sb_ref, n + 1, True)
        return carry

    def last_pair(first_ref, second_ref, n):
        score_matmuls(second_ref, n + 1)
        softmax_update(first_ref, n, False)
        softmax_update(second_ref, n + 1, False)

    def far_quad(q, carry):
        far_pair(2 * q, carry)
        return far_pair(2 * q + 1, carry)

    n_far = jnp.maximum(c - 1, 0)
    far_pairs = n_far // 2
    far_quads = far_pairs // 2
    lax.fori_loop(0, far_quads, far_quad, 0)
    lax.fori_loop(2 * far_quads, far_pairs, far_pair, 0)
    n0 = 2 * far_pairs
    odd_far = n_far % 2 == 1

    @pl.when(jnp.logical_not(odd_far))
    def _():
        last_pair(sa_ref, sb_ref, n0)

    @pl.when(odd_far)
    def _():
        score_matmuls(sb_ref, n0 + 1)
        softmax_update(sa_ref, n0, True)
        last_pair(sb_ref, sa_ref, n0 + 1)

    for j in range(n_heads):
        out_t = acc_ref[j] / l_ref[j]
        o_ref[0, qrows, cols(j)] = out_t.T.astype(o_ref.dtype)


def _moba_attention(qkv, bias_tiles):
    hpb = MOBA_HEADS_PER_STEP
    ng = N_HEADS // hpb
    gw = hpb * HEAD_DIM
    bsz, _, s, _ = qkv.shape
    assert qkv.shape == (bsz, 3 * ng, s, gw)
    nb = s // MOBA_BLOCK
    return pl.pallas_call(
        _moba_kernel,
        name="moba_attention",
        grid=(bsz, ng),
        in_specs=[pl.BlockSpec((None, None, s, gw), lambda b, g: (b, g, 0, 0)),
                  pl.BlockSpec((None, None, s, gw), lambda b, g: (b, ng + g, 0, 0)),
                  pl.BlockSpec((None, None, s, gw), lambda b, g: (b, 2 * ng + g, 0, 0)),
                  pl.BlockSpec((hpb, 3, MOBA_BLOCK, MOBA_BLOCK), lambda b, g: (g, 0, 0, 0))],
        out_specs=pl.BlockSpec((1, s, gw), lambda b, g: (b, 0, g)),
        out_shape=jax.ShapeDtypeStruct((bsz, s, ATT_WIDTH), jnp.bfloat16),
        scratch_shapes=[pltpu.VMEM((hpb, nb, HEAD_DIM), jnp.float32),
                        pltpu.VMEM((hpb, 3 * nb, HEAD_DIM), jnp.bfloat16),
                        pltpu.VMEM((hpb, nb, HEAD_DIM, MOBA_BLOCK), jnp.bfloat16),
                        pltpu.VMEM((hpb, nb, MOBA_BLOCK), jnp.float32),
                        pltpu.VMEM((hpb, 1, MOBA_BLOCK), jnp.float32),
                        pltpu.VMEM((hpb, 1, MOBA_BLOCK), jnp.float32),
                        pltpu.VMEM((hpb, HEAD_DIM, MOBA_BLOCK), jnp.float32),
                        pltpu.VMEM((hpb, MOBA_BLOCK, MOBA_BLOCK), jnp.float32),
                        pltpu.VMEM((hpb, MOBA_BLOCK, MOBA_BLOCK), jnp.float32)],
        compiler_params=pltpu.CompilerParams(
            dimension_semantics=("parallel", "parallel"), vmem_limit_bytes=VMEM_LIMIT),
    )(qkv, qkv, qkv, bias_tiles)


def _bias_by_bucket(rel_bias, bucket):
    onehot = (bucket[None] == jnp.arange(NUM_BUCKETS).reshape((NUM_BUCKETS,) + (1,) * bucket.ndim))
    return jnp.einsum('kh,k...->h...', rel_bias.astype(jnp.float32), onehot.astype(jnp.float32),
                      precision=lax.Precision.HIGHEST)


def _moba_bias_tables(rel_bias):
    key = jnp.arange(MOBA_BLOCK)[:, None]
    qry = jnp.arange(MOBA_BLOCK)[None, :]
    d_own = qry - key
    own = jnp.where(d_own >= 0, _bias_by_bucket(rel_bias, _t5_bucket(d_own)), NEG)
    prev = _bias_by_bucket(rel_bias, _t5_bucket(d_own + MOBA_BLOCK))
    far = jnp.broadcast_to(rel_bias.astype(jnp.float32)[NUM_BUCKETS - 1][:, None, None], own.shape)
    return jnp.stack([far, prev, own], axis=1) * LOG2E


def _swa_kernel(q_ref, kvp_ref, kvc_ref, bias_ref, sink_ref, o_ref):
    c = pl.program_id(1)
    blk = SWA_BLOCK
    n_sub = q_ref.shape[1] // blk
    rows = lax.broadcasted_iota(jnp.int32, (HEAD_DIM, HEAD_DIM), 0)
    lanes = lax.broadcasted_iota(jnp.int32, (HEAD_DIM, HEAD_DIM), 1)
    eye = (rows == lanes).astype(jnp.bfloat16)
    head = lambda hh: slice(hh * HEAD_DIM, (hh + 1) * HEAD_DIM)
    kv_cols = lambda g, is_v: slice((is_v * N_KV_HEADS + g) * HEAD_DIM, (is_v * N_KV_HEADS + g + 1) * HEAD_DIM)
    qrows = lambda t: slice(t * blk, (t + 1) * blk)

    def kv_block(t, g, is_v):
        return kvp_ref[0, :, kv_cols(g, is_v)] if t == 0 else kvc_ref[0, qrows(t - 1), kv_cols(g, is_v)]

    values = [[lax.dot_general(eye, kv_block(t, g, 1), _NT,
                               preferred_element_type=jnp.float32).astype(jnp.bfloat16)
               for g in range(N_KV_HEADS)] for t in range(n_sub + 1)]

    pairs = [(t, g) for t in range(n_sub) for g in range(N_KV_HEADS)]
    scores = []
    for t, g in pairs:
        qg = jnp.concatenate([q_ref[0, qrows(t), head(g * Q_GROUP + i)] for i in range(Q_GROUP)], axis=0)
        scores.append(tuple(lax.dot_general(kv_block(t + d, g, 0), qg, _NT, preferred_element_type=jnp.float32)
                            for d in (0, 1)))

    probs = []
    for (t, g), (raw_prev, raw_cur) in zip(pairs, scores):
        has_prev = jnp.logical_or(c > 0, t > 0)
        s_prev = jnp.where(has_prev, raw_prev + bias_ref[g, :blk, :], NEG)
        s_cur = raw_cur + bias_ref[g, blk:, :]
        sink = sink_ref[g]
        m = jnp.maximum(jnp.maximum(jnp.max(s_prev, axis=0, keepdims=True),
                                    jnp.max(s_cur, axis=0, keepdims=True)), sink)
        p_prev = jnp.exp2(s_prev - m)
        p_cur = jnp.exp2(s_cur - m)
        denom = (jnp.sum(p_prev, axis=0, keepdims=True) + jnp.sum(p_cur, axis=0, keepdims=True)
                 + jnp.exp2(sink - m))
        probs.append((p_prev.astype(jnp.bfloat16), p_cur.astype(jnp.bfloat16), denom))

    for (t, g), (p_prev, p_cur, denom) in zip(pairs, probs):
        out_t = (jnp.dot(values[t][g], p_prev, preferred_element_type=jnp.float32)
                 + jnp.dot(values[t + 1][g], p_cur, preferred_element_type=jnp.float32)) / denom
        for i in range(Q_GROUP):
            o_ref[0, qrows(t), head(g * Q_GROUP + i)] = out_t[:, i * blk:(i + 1) * blk].T.astype(o_ref.dtype)


def _swa_attention(q, kv, band_bias, sinks):
    bsz, s, _ = q.shape
    gq = Q_GROUP * SWA_BLOCK
    rows = SWA_BLOCKS_PER_STEP * SWA_BLOCK
    prev = lambda c: jnp.maximum(c * SWA_BLOCKS_PER_STEP - 1, 0)
    return pl.pallas_call(
        _swa_kernel,
        name="swa_attention",
        grid=(bsz, s // rows),
        in_specs=[pl.BlockSpec((1, rows, ATT_WIDTH), lambda b, c: (b, c, 0)),
                  pl.BlockSpec((1, SWA_BLOCK, 2 * KV_WIDTH), lambda b, c: (b, prev(c), 0)),
                  pl.BlockSpec((1, rows, 2 * KV_WIDTH), lambda b, c: (b, c, 0)),
                  pl.BlockSpec((N_KV_HEADS, 2 * SWA_BLOCK, gq), lambda b, c: (0, 0, 0)),
                  pl.BlockSpec((N_KV_HEADS, 1, gq), lambda b, c: (0, 0, 0))],
        out_specs=pl.BlockSpec((1, rows, ATT_WIDTH), lambda b, c: (b, c, 0)),
        out_shape=jax.ShapeDtypeStruct((bsz, s, ATT_WIDTH), jnp.bfloat16),
        compiler_params=pltpu.CompilerParams(
            dimension_semantics=("parallel", "arbitrary"), vmem_limit_bytes=VMEM_LIMIT),
    )(q, kv, kv, band_bias, sinks)


def _swa_band_bias(rel_bias):
    qi = jnp.arange(SWA_BLOCK)[:, None]
    kj = jnp.arange(2 * SWA_BLOCK)[None, :]
    dist = qi + SWA_BLOCK - kj
    in_band = (dist >= 0) & (dist < SWA_WINDOW)
    bias = jnp.where(in_band, _bias_by_bucket(rel_bias, _t5_bucket(dist)) * LOG2E, NEG)
    bias = bias.reshape(N_KV_HEADS, Q_GROUP, SWA_BLOCK, 2 * SWA_BLOCK)
    return bias.transpose(0, 3, 1, 2).reshape(N_KV_HEADS, 2 * SWA_BLOCK, Q_GROUP * SWA_BLOCK)


def _swa_sink_rows(sinks):
    rows = jnp.repeat(sinks.astype(jnp.float32) * LOG2E, SWA_BLOCK)
    return rows.reshape(N_KV_HEADS, 1, Q_GROUP * SWA_BLOCK)


def _pad_ffn_weights(w_up, conv_w, conv_b, w_down):
    pad = D_FF_PAD - D_FF

    def halves(a, dtype):
        return tuple(jnp.pad(part.astype(dtype), ((0, 0), (0, 0), (0, pad)))
                     for part in (a[..., :D_FF], a[..., D_FF:]))

    return (halves(w_up, jnp.bfloat16) + halves(conv_w, jnp.float32) + halves(conv_b[:, None, :], jnp.float32)
            + (jnp.pad(w_down.astype(jnp.bfloat16), ((0, 0), (0, pad), (0, 0))),))


def kernel(x, rel_bias, moba_w_qkv, moba_w_o, swa_w_kv, swa_w_q, swa_sinks, swa_w_o,
           ffn_w_up, ffn_conv_w, ffn_conv_b, ffn_w_down, ln_g, ln_b):
    bsz, s, d = x.shape
    bf16 = jnp.bfloat16
    moba_bias = _moba_bias_tables(rel_bias)
    band_bias = _swa_band_bias(rel_bias)
    w_qkv, w_o_moba = moba_w_qkv.astype(bf16), moba_w_o.astype(bf16)
    w_kv, w_q, w_o_swa = swa_w_kv.astype(bf16)[None], swa_w_q.astype(bf16), swa_w_o.astype(bf16)
    ffn_weights = _pad_ffn_weights(ffn_w_up, ffn_conv_w, ffn_conv_b, ffn_w_down)

    h = x.reshape(bsz * s, d)
    kv = None
    for layer in range(DEPTH):
        if layer < N_A_LAYERS:
            qkv = _proj(h, w_qkv, layer, scaled_cols=ATT_WIDTH, scale=SCALE * LOG2E,
                        group=(bsz, s, MOBA_HEADS_PER_STEP * HEAD_DIM))
            attn = _moba_attention(qkv, moba_bias)
            w_o, j = w_o_moba, layer
        else:
            j = layer - N_A_LAYERS
            if kv is None:
                kv = _proj(h, w_kv, 0).reshape(bsz, s, 2 * KV_WIDTH)
            q = _proj(h, w_q, j, scaled_cols=ATT_WIDTH, scale=SCALE * LOG2E).reshape(bsz, s, ATT_WIDTH)
            attn = _swa_attention(q, kv, band_bias, _swa_sink_rows(swa_sinks[j]))
            w_o = w_o_swa
        h = _proj_ln(attn.reshape(bsz * s, ATT_WIDTH), w_o, j, h,
                     ln_g[layer, 0][None, :], ln_b[layer, 0][None, :])
        h = _conv_ffn_ln(h, ffn_weights, layer,
                         ln_g[layer, 1][None, :], ln_b[layer, 1][None, :], s)
    return h.reshape(bsz, s, d)
```
